```python
import math
import jax, jax.numpy as jnp
from jax import lax
import numpy as np

D_MODEL = 2048
BATCH = 4
SEQ = 2048
DEPTH = 4
DEC_BATCH = 128
DEC_SEQ = 8
PAST_LEN = 16384
PAGE_SIZE = 128

CHUNK = 128
N_HEADS_A = 8
HEAD_DIM_A = 128
W_A = N_HEADS_A * HEAD_DIM_A
N_GROUPS_B = 8
W_B = D_MODEL - W_A
CONV_W = 3
D_FF = 5632
N_MOD = 6
IN_COLS = 2 * W_A + 3 * W_B
EPS = 1e-6

kernel_name = "hymba_style_gmlp_shortconv_convffn_step"


def rmsnorm(x, g):
    xf = x.astype(jnp.float32)
    y = xf * lax.rsqrt(jnp.mean(xf * xf, axis=-1, keepdims=True) + EPS)
    return (y * g.astype(jnp.float32)).astype(x.dtype)


def causal_conv3(x, past, w, b=None):
    xp = jnp.concatenate([past.astype(x.dtype), x], axis=1)
    y = w[0] * xp[:, :-2] + w[1] * xp[:, 1:-1] + w[2] * xp[:, 2:]
    if b is not None:
        y = y + b
    return y, xp[:, -(CONV_W - 1):]


def chunk_spatial_gate(z, g_v, w_s, b_s):
    bsz, t_len, _ = z.shape
    u, v = z[..., :W_A], z[..., W_A:]
    v = rmsnorm(v, g_v)
    blk = CHUNK if t_len >= CHUNK else t_len
    n_c = -(-t_len // blk)
    t_pad = n_c * blk
    vp = jnp.pad(v, ((0, 0), (0, t_pad - t_len), (0, 0)))
    vp = vp.reshape(bsz, n_c, blk, N_HEADS_A, HEAD_DIM_A)
    mask = jnp.tril(jnp.ones((blk, blk), dtype=bool))
    w = jnp.where(mask[None], w_s[:, :blk, :blk], jnp.zeros((), w_s.dtype))
    mixed = jnp.einsum('hts,bnshd->bnthd', w, vp)
    mixed = mixed + jnp.transpose(b_s[:, :blk])[None, None, :, :, None]
    mixed = mixed.reshape(bsz, t_pad, W_A)[:, :t_len]
    start = ((t_len - 1) // CHUNK) * CHUNK
    return u * mixed, v[:, start:]


def layer(x, c, past_mix, past_ffn, w_ada, b_ada, g_pre_mix, g_post_mix, g_pre_ffn, g_post_ffn,
          w_in, g_v, w_spatial, b_spatial, w_conv_mix, g_out_a, g_out_b, w_out,
          w_up, w_conv_ffn, b_conv_ffn, w_down):
    mod = jax.nn.silu(c) @ w_ada + b_ada
    sh1, sc1, gt1, sh2, sc2, gt2 = jnp.split(mod[:, None, :], N_MOD, axis=-1)

    h = rmsnorm(x, g_pre_mix) * (1 + sc1) + sh1
    proj = h @ w_in
    z_a = proj[..., :2 * W_A]
    gate_b = proj[..., 2 * W_A:2 * W_A + W_B]
    gate_c = proj[..., 2 * W_A + W_B:2 * W_A + 2 * W_B]
    h_b = proj[..., 2 * W_A + 2 * W_B:]
    out_a, v_rows = chunk_spatial_gate(jax.nn.gelu(z_a), g_v, w_spatial, b_spatial)
    conv_out, new_mix = causal_conv3(gate_c * h_b, past_mix, w_conv_mix)
    out_b = gate_b * conv_out
    merged = jnp.concatenate([rmsnorm(out_a, g_out_a), rmsnorm(out_b, g_out_b)], axis=-1) @ w_out
    x = x + gt1 * rmsnorm(merged, g_post_mix)

    h = rmsnorm(x, g_pre_ffn) * (1 + sc2) + sh2
    up = h @ w_up
    up_c, new_ffn = causal_conv3(up, past_ffn, w_conv_ffn, b_conv_ffn)
    ff = (jax.nn.silu(up_c[..., :D_FF]) * up_c[..., D_FF:]) @ w_down
    x = x + gt2 * rmsnorm(ff, g_post_ffn)
    return x, new_mix, new_ffn, v_rows


def run_trunk(x, c, past_mix, past_ffn, w_ada, b_ada, g_pre_mix, g_post_mix, g_pre_ffn, g_post_ffn,
              w_in, g_v, w_spatial, b_spatial, w_conv_mix, g_out_a, g_out_b, w_out,
              w_up, w_conv_ffn, b_conv_ffn, w_down):
    mixes, ffns, vrows = [], [], []
    for l in range(DEPTH):
        x, nm, nf, vr = layer(x, c, past_mix[:, l], past_ffn[:, l], w_ada[l], b_ada[l],
                              g_pre_mix[l], g_post_mix[l], g_pre_ffn[l], g_post_ffn[l],
                              w_in[l], g_v[l], w_spatial[l], b_spatial[l], w_conv_mix[l],
                              g_out_a[l], g_out_b[l], w_out[l],
                              w_up[l], w_conv_ffn[l], b_conv_ffn[l], w_down[l])
        mixes.append(nm)
        ffns.append(nf)
        vrows.append(vr)
    return x, jnp.stack(mixes, axis=1), jnp.stack(ffns, axis=1), jnp.stack(vrows, axis=1)


def setup_inputs(seed: int = 0) -> dict:
    key = jax.random.key(seed)
    ks = jax.random.split(key, 26)
    f32 = jnp.float32
    nrm = lambda k, shape, s: jax.random.normal(k, shape, f32) * s
    gain = lambda k, shape: 1.0 + 0.05 * jax.random.normal(k, shape, f32)
    return {
        "x_prompt": nrm(ks[0], (BATCH, SEQ, D_MODEL), 1.0),
        "x_sample": nrm(ks[1], (DEC_BATCH, DEC_SEQ, D_MODEL), 1.0),
        "state_conv_mix": nrm(ks[2], (DEC_BATCH, DEPTH, CONV_W - 1, W_B), 1.0),
        "state_conv_ffn": nrm(ks[3], (DEC_BATCH, DEPTH, CONV_W - 1, 2 * D_FF), 1.0),
        "c_prompt": nrm(ks[4], (BATCH, D_MODEL), 1.0),
        "c_sample": nrm(ks[5], (DEC_BATCH, D_MODEL), 1.0),
        "w_ada": nrm(ks[6], (DEPTH, D_MODEL, N_MOD * D_MODEL), 0.5 * D_MODEL ** -0.5),
        "b_ada": nrm(ks[7], (DEPTH, N_MOD * D_MODEL), 0.02),
        "g_pre_mix": gain(ks[8], (DEPTH, D_MODEL)),
        "g_post_mix": gain(ks[9], (DEPTH, D_MODEL)),
        "g_pre_ffn": gain(ks[10], (DEPTH, D_MODEL)),
        "g_post_ffn": gain(ks[11], (DEPTH, D_MODEL)),
        "w_in": nrm(ks[12], (DEPTH, D_MODEL, IN_COLS), D_MODEL ** -0.5),
        "g_v": gain(ks[13], (DEPTH, W_A)),
        "w_spatial": nrm(ks[14], (DEPTH, N_HEADS_A, CHUNK, CHUNK), 0.5 * CHUNK ** -0.5),
        "b_spatial": 1.0 + nrm(ks[15], (DEPTH, N_HEADS_A, CHUNK), 0.02),
        "w_conv_mix": nrm(ks[16], (DEPTH, CONV_W, W_B), CONV_W ** -0.5),
        "g_out_a": gain(ks[17], (DEPTH, W_A)),
        "g_out_b": gain(ks[18], (DEPTH, W_B)),
        "w_out": nrm(ks[19], (DEPTH, W_A + W_B, D_MODEL), (W_A + W_B) ** -0.5),
        "w_up": nrm(ks[20], (DEPTH, D_MODEL, 2 * D_FF), D_MODEL ** -0.5),
        "w_conv_ffn": nrm(ks[21], (DEPTH, CONV_W, 2 * D_FF), CONV_W ** -0.5),
        "b_conv_ffn": nrm(ks[22], (DEPTH, 2 * D_FF), 0.02),
        "w_down": nrm(ks[23], (DEPTH, D_FF, D_MODEL), D_FF ** -0.5),
    }


def reference(x_prompt, x_sample, state_conv_mix, state_conv_ffn, c_prompt, c_sample,
              w_ada, b_ada, g_pre_mix, g_post_mix, g_pre_ffn, g_post_ffn,
              w_in, g_v, w_spatial, b_spatial, w_conv_mix, g_out_a, g_out_b, w_out,
              w_up, w_conv_ffn, b_conv_ffn, w_down):
    weights = (w_ada, b_ada, g_pre_mix, g_post_mix, g_pre_ffn, g_post_ffn,
               w_in, g_v, w_spatial, b_spatial, w_conv_mix, g_out_a, g_out_b, w_out,
               w_up, w_conv_ffn, b_conv_ffn, w_down)
    bp = x_prompt.shape[0]
    zero_mix = jnp.zeros((bp, DEPTH, CONV_W - 1, W_B), x_prompt.dtype)
    zero_ffn = jnp.zeros((bp, DEPTH, CONV_W - 1, 2 * D_FF), x_prompt.dtype)
    y_prompt, mix_p, ffn_p, vrows_p = run_trunk(x_prompt, c_prompt, zero_mix, zero_ffn, *weights)
    y_sample, mix_s, ffn_s, vrows_s = run_trunk(x_sample, c_sample, state_conv_mix, state_conv_ffn, *weights)
    return (y_prompt, y_sample, mix_p, mix_s, ffn_p, ffn_s, vrows_p, vrows_s)
```

```python
import functools

import jax
import jax.numpy as jnp
from jax import lax
from jax.experimental import pallas as pl
from jax.experimental.pallas import tpu as pltpu

F32 = jnp.float32
BF16 = jnp.bfloat16

D_MODEL = 2048
DEPTH = 4
CHUNK = 128
N_HEADS_A = 8
HEAD_DIM_A = 128
W_A = N_HEADS_A * HEAD_DIM_A
W_B = D_MODEL - W_A
D_FF = 5632
N_MOD = 6
IN_COLS = 2 * W_A + 3 * W_B
EPS = 1e-6

MIX_ROWS = 256
FFN_ROWS = 512
FFN_COLS = 256
ADA_COLS = 1024
VMEM_LIMIT_BYTES = 56 * 1024 * 1024


def _rms(x, g):
    ms = jnp.mean(x * x, axis=-1, keepdims=True)
    return x * lax.rsqrt(ms + EPS) * g


def _gelu(x):
    c = 0.7978845608028654
    return 0.5 * x * (1.0 + jnp.tanh(c * (x + 0.044715 * (x * x * x))))


def _silu(x):
    return x * jax.nn.sigmoid(x)


def _dot(a, b):
    return jnp.dot(a, b, preferred_element_type=F32)


def _conv_rows(u, prev2, prev1, w_ref, b_ref):
    rows = lax.broadcasted_iota(jnp.int32, u.shape, 0)
    p1 = jnp.where(rows == 0, prev1, pltpu.roll(u, 1, 0))
    p2 = jnp.where(rows == 0, prev2, jnp.where(rows == 1, prev1, pltpu.roll(u, 2, 0)))
    y = w_ref[0:1, :] * p2 + w_ref[1:2, :] * p1 + w_ref[2:3, :] * u
    if b_ref is not None:
        y = y + b_ref[...]
    return y


def _conv_blocks(u, past0, past1, w_ref, b_ref, bb):
    t = u.shape[0]
    p1 = jnp.concatenate([past1, u[: t - bb]], axis=0)
    p2 = jnp.concatenate([past0, past1, u[: t - 2 * bb]], axis=0)
    y = w_ref[0:1, :] * p2 + w_ref[1:2, :] * p1 + w_ref[2:3, :] * u
    if b_ref is not None:
        y = y + b_ref[...]
    return y


def _ada_body(c_ref, w_ref, b_ref, o_ref):
    s = _silu(c_ref[...]).astype(BF16)
    o_ref[...] = _dot(s, w_ref[...].astype(BF16)) + b_ref[...]


def _ada_call(c_all, w_ada, b_ada):
    rows = c_all.shape[0]
    n_col = (N_MOD * D_MODEL) // ADA_COLS
    return pl.pallas_call(
        _ada_body,
        grid=(DEPTH, n_col),
        in_specs=[
            pl.BlockSpec((rows, D_MODEL), lambda l, n: (0, 0)),
            pl.BlockSpec((None, D_MODEL, ADA_COLS), lambda l, n: (l, 0, n)),
            pl.BlockSpec((None, 1, ADA_COLS), lambda l, n: (l, 0, n)),
        ],
        out_specs=pl.BlockSpec((None, rows, ADA_COLS), lambda l, n: (l, 0, n)),
        out_shape=jax.ShapeDtypeStruct((DEPTH, rows, N_MOD * D_MODEL), F32),
        compiler_params=pltpu.CompilerParams(
            dimension_semantics=("arbitrary", "arbitrary"),
            vmem_limit_bytes=VMEM_LIMIT_BYTES),
        name="ada_mod",
    )(c_all, w_ada, b_ada.reshape(DEPTH, 1, N_MOD * D_MODEL))


def _mixer_core(x, sh, sc, gt, gpre_ref, w_in_ref, g_v_ref, g_oa_ref, g_ob_ref, w_out_ref,
                gpost_ref, oa_s, spatial_fn, conv_fn):
    h = (_rms(x, gpre_ref[...]) * (1.0 + sc) + sh).astype(BF16)
    u = _gelu(_dot(h, w_in_ref[:, 0:W_A]))
    v = _rms(_gelu(_dot(h, w_in_ref[:, W_A:2 * W_A])), g_v_ref[...])
    spatial_fn(u, v)
    ra = _rms(oa_s[...], g_oa_ref[...]).astype(BF16)
    gate_b = _dot(h, w_in_ref[:, 2 * W_A:2 * W_A + W_B])
    gate_c = _dot(h, w_in_ref[:, 2 * W_A + W_B:2 * W_A + 2 * W_B])
    h_b = _dot(h, w_in_ref[:, 2 * W_A + 2 * W_B:])
    ci = gate_c * h_b
    rb = _rms(gate_b * conv_fn(ci), g_ob_ref[...]).astype(BF16)
    merged = _dot(ra, w_out_ref[0:W_A, :]) + _dot(rb, w_out_ref[W_A:, :])
    y = x + gt * _rms(merged, gpost_ref[...])
    return y, v, ci


def _mixer_prompt_body(tiles_per_seq, x_ref, sh_ref, sc_ref, gt_ref, gpre_ref, w_in_ref, g_v_ref,
                       wsp_ref, bsp_ref, wconv_ref, g_oa_ref, g_ob_ref, w_out_ref, gpost_ref,
                       y_ref, nmix_ref, vrows_ref, oa_s, carry_s):
    i = pl.program_id(0)
    tm = x_ref.shape[0]
    first = (i % tiles_per_seq) == 0

    def spatial_fn(u, v):
        vb = v.astype(BF16)
        tril = (lax.broadcasted_iota(jnp.int32, (CHUNK, CHUNK), 0)
                >= lax.broadcasted_iota(jnp.int32, (CHUNK, CHUNK), 1))
        for hd in range(N_HEADS_A):
            cols = slice(hd * HEAD_DIM_A, (hd + 1) * HEAD_DIM_A)
            wm = jnp.where(tril, wsp_ref[hd], 0.0).astype(BF16)
            bias = bsp_ref[:, hd:hd + 1]
            for c in range(tm // CHUNK):
                rows = slice(c * CHUNK, (c + 1) * CHUNK)
                mixed = _dot(wm, vb[rows, cols]) + bias
                oa_s[rows, cols] = u[rows, cols] * mixed

    def conv_fn(ci):
        prev2 = jnp.where(first, 0.0, carry_s[6:7, :])
        prev1 = jnp.where(first, 0.0, carry_s[7:8, :])
        return _conv_rows(ci, prev2, prev1, wconv_ref, None)

    y, v, ci = _mixer_core(x_ref[...], sh_ref[...], sc_ref[...], gt_ref[...], gpre_ref, w_in_ref,
                           g_v_ref, g_oa_ref, g_ob_ref, w_out_ref, gpost_ref, oa_s,
                           spatial_fn, conv_fn)
    y_ref[...] = y
    carry_s[...] = ci[tm - 8:, :]
    nmix_ref[...] = ci[tm - 2:, :]
    vrows_ref[...] = v[tm - CHUNK:, :]


def _mixer_sample_body(layer, x_ref, sh_ref, sc_ref, gt_ref, gpre_ref, w_in_ref, g_v_ref,
                       wsp_ref, bsp_ref, wconv_ref, past0_ref, past1_ref, g_oa_ref, g_ob_ref,
                       w_out_ref, gpost_ref, y_ref, nmix_ref, vrows_ref, oa_s):
    n_t, bb, _ = x_ref.shape
    tm = n_t * bb

    def rep(m_ref):
        return jnp.concatenate([m_ref[...]] * n_t, axis=0)

    def spatial_fn(u, v):
        for hd in range(N_HEADS_A):
            cols = slice(hd * HEAD_DIM_A, (hd + 1) * HEAD_DIM_A)
            base = (layer * N_HEADS_A + hd) * n_t
            for t in range(n_t):
                mixed = None
                for s in range(t + 1):
                    term = wsp_ref[(base + t) * n_t + s] * v[s * bb:(s + 1) * bb, cols]
                    mixed = term if mixed is None else mixed + term
                mixed = mixed + bsp_ref[base + t]
                oa_s[t * bb:(t + 1) * bb, cols] = u[t * bb:(t + 1) * bb, cols] * mixed

    def conv_fn(ci):
        return _conv_blocks(ci, past0_ref[...], past1_ref[...], wconv_ref, None, bb)

    x = x_ref[...].reshape(tm, D_MODEL)
    y, v, ci = _mixer_core(x, rep(sh_ref), rep(sc_ref), rep(gt_ref), gpre_ref, w_in_ref,
                           g_v_ref, g_oa_ref, g_ob_ref, w_out_ref, gpost_ref, oa_s,
                           spatial_fn, conv_fn)
    y_ref[...] = y.reshape(n_t, bb, D_MODEL)
    nmix_ref[...] = ci[tm - 2 * bb:, :].reshape(2, bb, W_B)
    vrows_ref[...] = v.reshape(n_t, bb, W_A)


def _layer_vec(arr, l):
    c = arr.shape[-1]
    return arr.reshape(DEPTH, 1, c), pl.BlockSpec((None, 1, c), lambda *_: (l, 0, 0))


def _resident(shape, index):
    return pl.BlockSpec(shape, lambda *_: index, pipeline_mode=pl.Buffered(1))


def _mixer_prompt_call(l, x, mod_p, g_pre, w_in_bf, g_v, w_spatial, b_spatial_t, w_conv, g_oa, g_ob,
                       w_out_bf, g_post):
    n_tok = x.shape[0]
    n_seq = mod_p.shape[1]
    seq = n_tok // n_seq
    tm = MIX_ROWS
    tps = seq // tm

    def mod_spec(k):
        return pl.BlockSpec((None, None, 1, D_MODEL), lambda i: (l, i // tps, 0, k))

    gpre_a, gpre_s = _layer_vec(g_pre, l)
    gv_a, gv_s = _layer_vec(g_v, l)
    goa_a, goa_s = _layer_vec(g_oa, l)
    gob_a, gob_s = _layer_vec(g_ob, l)
    gpost_a, gpost_s = _layer_vec(g_post, l)
    return pl.pallas_call(
        functools.partial(_mixer_prompt_body, tps),
        grid=(n_tok // tm,),
        in_specs=[
            pl.BlockSpec((tm, D_MODEL), lambda i: (i, 0)),
            mod_spec(0), mod_spec(1), mod_spec(2),
            gpre_s,
            _resident((None, D_MODEL, IN_COLS), (l, 0, 0)),
            gv_s,
            pl.BlockSpec((None, N_HEADS_A, CHUNK, CHUNK), lambda i: (l, 0, 0, 0)),
            pl.BlockSpec((None, CHUNK, N_HEADS_A), lambda i: (l, 0, 0)),
            pl.BlockSpec((None, 3, W_B), lambda i: (l, 0, 0)),
            goa_s, gob_s,
            _resident((None, D_MODEL, D_MODEL), (l, 0, 0)),
            gpost_s,
        ],
        out_specs=[
            pl.BlockSpec((tm, D_MODEL), lambda i: (i, 0)),
            pl.BlockSpec((None, 2, W_B), lambda i: (i // tps, 0, 0)),
            pl.BlockSpec((None, CHUNK, W_A), lambda i: (i // tps, 0, 0)),
        ],
        out_shape=[
            jax.ShapeDtypeStruct((n_tok, D_MODEL), F32),
            jax.ShapeDtypeStruct((n_seq, 2, W_B), F32),
            jax.ShapeDtypeStruct((n_seq, CHUNK, W_A), F32),
        ],
        scratch_shapes=[pltpu.VMEM((tm, W_A), F32), pltpu.VMEM((8, W_B), F32)],
        compiler_params=pltpu.CompilerParams(
            dimension_semantics=("arbitrary",), vmem_limit_bytes=VMEM_LIMIT_BYTES),
        name="mixer_prompt",
    )(x, mod_p, mod_p, mod_p, gpre_a, w_in_bf, gv_a, w_spatial, b_spatial_t, w_conv, goa_a, gob_a,
      w_out_bf, gpost_a)


def _mixer_sample_call(l, x, mod, state_mix, g_pre, w_in_bf, g_v, wsp_flat, bsp_flat, w_conv, g_oa,
                       g_ob, w_out_bf, g_post):
    n_t, n_b, _ = x.shape
    bb = MIX_ROWS // n_t
    tm = n_t * bb

    def mod_spec(k):
        return pl.BlockSpec((None, bb, D_MODEL), lambda i: (l, i, k))

    def past_spec(r):
        return pl.BlockSpec((bb, W_B), lambda i: (i, l * 2 + r))

    gpre_a, gpre_s = _layer_vec(g_pre, l)
    gv_a, gv_s = _layer_vec(g_v, l)
    goa_a, goa_s = _layer_vec(g_oa, l)
    gob_a, gob_s = _layer_vec(g_ob, l)
    gpost_a, gpost_s = _layer_vec(g_post, l)
    smem = pl.BlockSpec(memory_space=pltpu.SMEM)
    return pl.pallas_call(
        functools.partial(_mixer_sample_body, l),
        grid=(n_b // bb,),
        in_specs=[
            pl.BlockSpec((n_t, bb, D_MODEL), lambda i: (0, i, 0)),
            mod_spec(0), mod_spec(1), mod_spec(2),
            gpre_s,
            _resident((None, D_MODEL, IN_COLS), (l, 0, 0)),
            gv_s,
            smem, smem,
            pl.BlockSpec((None, 3, W_B), lambda i: (l, 0, 0)),
            past_spec(0), past_spec(1),
            goa_s, gob_s,
            _resident((None, D_MODEL, D_MODEL), (l, 0, 0)),
            gpost_s,
        ],
        out_specs=[
            pl.BlockSpec((n_t, bb, D_MODEL), lambda i: (0, i, 0)),
            pl.BlockSpec((2, bb, W_B), lambda i: (0, i, 0)),
            pl.BlockSpec((n_t, bb, W_A), lambda i: (0, i, 0)),
        ],
        out_shape=[
            jax.ShapeDtypeStruct((n_t, n_b, D_MODEL), F32),
            jax.ShapeDtypeStruct((2, n_b, W_B), F32),
            jax.ShapeDtypeStruct((n_t, n_b, W_A), F32),
        ],
        scratch_shapes=[pltpu.VMEM((tm, W_A), F32)],
        compiler_params=pltpu.CompilerParams(
            dimension_semantics=("arbitrary",), vmem_limit_bytes=VMEM_LIMIT_BYTES),
        name="mixer_sample",
    )(x, mod, mod, mod, gpre_a, w_in_bf, gv_a, wsp_flat, bsp_flat, w_conv, state_mix, state_mix,
      goa_a, gob_a, w_out_bf, gpost_a)


def _ffn_step(j, n_j, x, sh, sc, gt, gpre_ref, wg_ref, wv_ref, wd_ref, gpost_ref, h_s, acc_ref,
              conv_g, conv_v, write_acc):
    @pl.when(j == 0)
    def _():
        h_s[...] = (_rms(x(), gpre_ref[...]) * (1.0 + sc()) + sh()).astype(BF16)

    h = h_s[...]
    up_g = _dot(h, wg_ref[...].astype(BF16))
    up_v = _dot(h, wv_ref[...].astype(BF16))
    act = (_silu(conv_g(up_g)) * conv_v(up_v)).astype(BF16)
    part = _dot(act, wd_ref[...].astype(BF16))

    @pl.when(j == 0)
    def _():
        acc_ref[...] = part

    @pl.when(j > 0)
    def _():
        acc_ref[...] += part

    @pl.when(j == n_j - 1)
    def _():
        write_acc(x() + gt() * _rms(acc_ref[...], gpost_ref[...]))

    return up_g, up_v


def _ffn_prompt_body(tiles_per_seq, x_ref, sh_ref, sc_ref, gt_ref, gpre_ref, wg_ref, wv_ref,
                     cwg_ref, cwv_ref, cbg_ref, cbv_ref, wd_ref, gpost_ref,
                     o_ref, nfg_ref, nfv_ref, h_s, carry_g, carry_v):
    i = pl.program_id(0)
    j = pl.program_id(1)
    n_j = pl.num_programs(1)
    tm = x_ref.shape[0]
    first = (i % tiles_per_seq) == 0

    def conv(carry, w_ref, b_ref):
        def fn(u):
            prev = carry[j]
            prev2 = jnp.where(first, 0.0, prev[6:7, :])
            prev1 = jnp.where(first, 0.0, prev[7:8, :])
            return _conv_rows(u, prev2, prev1, w_ref, b_ref)
        return fn

    def write_acc(val):
        o_ref[...] = val

    up_g, up_v = _ffn_step(
        j, n_j, lambda: x_ref[...], lambda: sh_ref[...], lambda: sc_ref[...], lambda: gt_ref[...],
        gpre_ref, wg_ref, wv_ref, wd_ref, gpost_ref, h_s, o_ref,
        conv(carry_g, cwg_ref, cbg_ref), conv(carry_v, cwv_ref, cbv_ref), write_acc)
    carry_g[j] = up_g[tm - 8:, :]
    carry_v[j] = up_v[tm - 8:, :]
    nfg_ref[...] = up_g[tm - 2:, :]
    nfv_ref[...] = up_v[tm - 2:, :]


def _ffn_sample_body(x_ref, sh_ref, sc_ref, gt_ref, gpre_ref, wg_ref, wv_ref,
                     cwg_ref, cwv_ref, cbg_ref, cbv_ref, pg0_ref, pg1_ref, pv0_ref, pv1_ref,
                     wd_ref, gpost_ref, o_ref, nfg_ref, nfv_ref, h_s, acc_s):
    j = pl.program_id(1)
    n_j = pl.num_programs(1)
    n_t, bb, _ = x_ref.shape
    tm = n_t * bb

    def rep(m_ref):
        return lambda: jnp.concatenate([m_ref[...]] * n_t, axis=0)

    def conv(p0_ref, p1_ref, w_ref, b_ref):
        return lambda u: _conv_blocks(u, p0_ref[...], p1_ref[...], w_ref, b_ref, bb)

    def write_acc(val):
        o_ref[...] = val.reshape(n_t, bb, D_MODEL)

    up_g, up_v = _ffn_step(
        j, n_j, lambda: x_ref[...].reshape(tm, D_MODEL), rep(sh_ref), rep(sc_ref), rep(gt_ref),
        gpre_ref, wg_ref, wv_ref, wd_ref, gpost_ref, h_s, acc_s,
        conv(pg0_ref, pg1_ref, cwg_ref, cbg_ref), conv(pv0_ref, pv1_ref, cwv_ref, cbv_ref),
        write_acc)
    fc = up_g.shape[1]
    nfg_ref[...] = up_g[tm - 2 * bb:, :].reshape(2, bb, fc)
    nfv_ref[...] = up_v[tm - 2 * bb:, :].reshape(2, bb, fc)


def _ffn_weight_specs(l, fc):
    n_j = D_FF // fc
    return [
        pl.BlockSpec((None, D_MODEL, fc), lambda i, j: (l, 0, j)),
        pl.BlockSpec((None, D_MODEL, fc), lambda i, j: (l, 0, n_j + j)),
        pl.BlockSpec((None, 3, fc), lambda i, j: (l, 0, j)),
        pl.BlockSpec((None, 3, fc), lambda i, j: (l, 0, n_j + j)),
        pl.BlockSpec((None, 1, fc), lambda i, j: (l, 0, j)),
        pl.BlockSpec((None, 1, fc), lambda i, j: (l, 0, n_j + j)),
    ]


def _ffn_prompt_call(l, x, mod_p, g_pre, w_up, w_conv, b_conv, w_down, g_post):
    n_tok = x.shape[0]
    n_seq = mod_p.shape[1]
    seq = n_tok // n_seq
    tm, fc = FFN_ROWS, FFN_COLS
    tps = seq // tm
    n_j = D_FF // fc

    def mod_spec(k):
        return pl.BlockSpec((None, None, 1, D_MODEL), lambda i, j: (l, i // tps, 0, k))

    gpre_a, gpre_s = _layer_vec(g_pre, l)
    gpost_a, gpost_s = _layer_vec(g_post, l)
    b3 = b_conv.reshape(DEPTH, 1, 2 * D_FF)
    return pl.pallas_call(
        functools.partial(_ffn_prompt_body, tps),
        grid=(n_tok // tm, n_j),
        in_specs=[
            pl.BlockSpec((tm, D_MODEL), lambda i, j: (i, 0)),
            mod_spec(3), mod_spec(4), mod_spec(5),
            gpre_s,
            *_ffn_weight_specs(l, fc),
            pl.BlockSpec((None, fc, D_MODEL), lambda i, j: (l, j, 0)),
            gpost_s,
        ],
        out_specs=[
            pl.BlockSpec((tm, D_MODEL), lambda i, j: (i, 0)),
            pl.BlockSpec((None, 2, fc), lambda i, j: (i, 0, j)),
            pl.BlockSpec((None, 2, fc), lambda i, j: (i, 0, j)),
        ],
        out_shape=[
            jax.ShapeDtypeStruct((n_tok, D_MODEL), F32),
            jax.ShapeDtypeStruct((n_tok // tm, 2, D_FF), F32),
            jax.ShapeDtypeStruct((n_tok // tm, 2, D_FF), F32),
        ],
        scratch_shapes=[
            pltpu.VMEM((tm, D_MODEL), BF16),
            pltpu.VMEM((n_j, 8, fc), F32),
            pltpu.VMEM((n_j, 8, fc), F32),
        ],
        compiler_params=pltpu.CompilerParams(
            dimension_semantics=("arbitrary", "arbitrary"), vmem_limit_bytes=VMEM_LIMIT_BYTES),
        name="ffn_prompt",
    )(x, mod_p, mod_p, mod_p, gpre_a, w_up, w_up, w_conv, w_conv, b3, b3, w_down, gpost_a)


def _ffn_sample_call(l, x, mod, state_ffn, g_pre, w_up, w_conv, b_conv, w_down, g_post):
    n_t, n_b, _ = x.shape
    fc = FFN_COLS
    bb = FFN_ROWS // n_t
    tm = n_t * bb
    n_j = D_FF // fc

    def mod_spec(k):
        return pl.BlockSpec((None, bb, D_MODEL), lambda i, j: (l, i, k))

    def past_spec(r, half):
        base = ((l * 2 + r) * 2 + half) * n_j
        return pl.BlockSpec((bb, fc), lambda i, j: (i, base + j))

    gpre_a, gpre_s = _layer_vec(g_pre, l)
    gpost_a, gpost_s = _layer_vec(g_post, l)
    b3 = b_conv.reshape(DEPTH, 1, 2 * D_FF)
    return pl.pallas_call(
        _ffn_sample_body,
        grid=(n_b // bb, n_j),
        in_specs=[
            pl.BlockSpec((n_t, bb, D_MODEL), lambda i, j: (0, i, 0)),
            mod_spec(3), mod_spec(4), mod_spec(5),
            gpre_s,
            *_ffn_weight_specs(l, fc),
            past_spec(0, 0), past_spec(1, 0), past_spec(0, 1), past_spec(1, 1),
            pl.BlockSpec((None, fc, D_MODEL), lambda i, j: (l, j, 0)),
            gpost_s,
        ],
        out_specs=[
            pl.BlockSpec((n_t, bb, D_MODEL), lambda i, j: (0, i, 0)),
            pl.BlockSpec((2, bb, fc), lambda i, j: (0, i, j)),
            pl.BlockSpec((2, bb, fc), lambda i, j: (0, i, j)),
        ],
        out_shape=[
            jax.ShapeDtypeStruct((n_t, n_b, D_MODEL), F32),
            jax.ShapeDtypeStruct((2, n_b, D_FF), F32),
            jax.ShapeDtypeStruct((2, n_b, D_FF), F32),
        ],
        scratch_shapes=[pltpu.VMEM((tm, D_MODEL), BF16), pltpu.VMEM((tm, D_MODEL), F32)],
        compiler_params=pltpu.CompilerParams(
            dimension_semantics=("arbitrary", "arbitrary"), vmem_limit_bytes=VMEM_LIMIT_BYTES),
        name="ffn_sample",
    )(x, mod, mod, mod, gpre_a, w_up, w_up, w_conv, w_conv, b3, b3,
      state_ffn, state_ffn, state_ffn, state_ffn, w_down, gpost_a)


def kernel(x_prompt, x_sample, state_conv_mix, state_conv_ffn, c_prompt, c_sample, w_ada, b_ada, g_pre_mix, g_post_mix, g_pre_ffn, g_post_ffn, w_in, g_v, w_spatial, b_spatial, w_conv_mix, g_out_a, g_out_b, w_out, w_up, w_conv_ffn, b_conv_ffn, w_down):
    n_seq, seq, _ = x_prompt.shape
    n_b, n_t, _ = x_sample.shape

    pad = (-(n_b + n_seq)) % 8
    c_all = jnp.concatenate([c_sample, c_prompt, jnp.zeros((pad, D_MODEL), F32)], axis=0)
    mod = _ada_call(c_all, w_ada, b_ada)
    mod_p = mod[:, n_b:n_b + n_seq].reshape(DEPTH, n_seq, 1, N_MOD * D_MODEL)

    w_in_bf = w_in.astype(BF16)
    w_out_bf = w_out.astype(BF16)
    b_spatial_t = jnp.swapaxes(b_spatial, 1, 2)
    wsp_flat = w_spatial[:, :, :n_t, :n_t].reshape(-1)
    bsp_flat = b_spatial[:, :, :n_t].reshape(-1)
    state_mix = state_conv_mix.reshape(n_b, DEPTH * 2 * W_B)
    state_ffn = state_conv_ffn.reshape(n_b, DEPTH * 2 * 2 * D_FF)

    xp = x_prompt.reshape(n_seq * seq, D_MODEL)
    xs = jnp.swapaxes(x_sample, 0, 1)
    mix_p, mix_s, ffn_p, ffn_s, vr_p, vr_s = [], [], [], [], [], []
    for l in range(DEPTH):
        xp, nm, vr = _mixer_prompt_call(l, xp, mod_p, g_pre_mix, w_in_bf, g_v, w_spatial,
                                        b_spatial_t, w_conv_mix, g_out_a, g_out_b, w_out_bf,
                                        g_post_mix)
        mix_p.append(nm)
        vr_p.append(vr)
        xs, nm, vr = _mixer_sample_call(l, xs, mod, state_mix, g_pre_mix, w_in_bf, g_v, wsp_flat,
                                        bsp_flat, w_conv_mix, g_out_a, g_out_b, w_out_bf,
                                        g_post_mix)
        mix_s.append(jnp.swapaxes(nm, 0, 1))
        vr_s.append(jnp.swapaxes(vr, 0, 1))
        xp, nfg, nfv = _ffn_prompt_call(l, xp, mod_p, g_pre_ffn, w_up, w_conv_ffn, b_conv_ffn,
                                        w_down, g_post_ffn)
        tps = nfg.shape[0] // n_seq
        ffn_p.append(jnp.concatenate([nfg, nfv], axis=-1)[tps - 1::tps])
        xs, nfg, nfv = _ffn_sample_call(l, xs, mod, state_ffn, g_pre_ffn, w_up, w_conv_ffn,
                                        b_conv_ffn, w_down, g_post_ffn)
        ffn_s.append(jnp.swapaxes(jnp.concatenate([nfg, nfv], axis=-1), 0, 1))

    return (xp.reshape(n_seq, seq, D_MODEL), jnp.swapaxes(xs, 0, 1),
            jnp.stack(mix_p, axis=1), jnp.stack(mix_s, axis=1),
            jnp.stack(ffn_p, axis=1), jnp.stack(ffn_s, axis=1),
            jnp.stack(vr_p, axis=1), jnp.stack(vr_s, axis=1))
```

```python
import functools

import jax
import jax.numpy as jnp
from jax import lax
from jax.experimental import pallas as pl
from jax.experimental.pallas import tpu as pltpu

F32 = jnp.float32
BF16 = jnp.bfloat16

D_MODEL = 2048
DEPTH = 4
CHUNK = 128
N_HEADS_A = 8
HEAD_DIM_A = 128
W_A = N_HEADS_A * HEAD_DIM_A
W_B = D_MODEL - W_A
D_FF = 5632
N_MOD = 6
IN_COLS = 2 * W_A + 3 * W_B
EPS = 1e-6

MIX_ROWS = 256
FFN_ROWS = 512
FFN_COLS = 512
NORM_ROWS = 128
CONV_ROWS = 64
FFN_ROW_BLOCK = 128
ADA_COLS = 1024
VMEM_LIMIT_BYTES = 56 * 1024 * 1024


def _rms(x, g):
    ms = jnp.mean(x * x, axis=-1, keepdims=True)
    return x * lax.rsqrt(ms + EPS) * g


def _gelu(x):
    c = 0.7978845608028654
    return 0.5 * x * (1.0 + jnp.tanh(c * (x + 0.044715 * (x * x * x))))


def _silu(x):
    return x * jax.nn.sigmoid(x)


def _dot(a, b):
    return jnp.dot(a, b, preferred_element_type=F32)


def _conv_rows(u, prev2, prev1, w_ref, b_ref):
    rows = lax.broadcasted_iota(jnp.int32, u.shape, 0)
    p1 = jnp.where(rows == 0, prev1, pltpu.roll(u, 1, 0))
    p2 = jnp.where(rows == 0, prev2, jnp.where(rows == 1, prev1, pltpu.roll(u, 2, 0)))
    y = w_ref[0:1, :] * p2 + w_ref[1:2, :] * p1 + w_ref[2:3, :] * u
    if b_ref is not None:
        y = y + b_ref[...]
    return y


def _conv_blocks(u, past0, past1, w_ref, b_ref, bb):
    t = u.shape[0]
    p1 = jnp.concatenate([past1, u[: t - bb]], axis=0)
    p2 = jnp.concatenate([past0, past1, u[: t - 2 * bb]], axis=0)
    y = w_ref[0:1, :] * p2 + w_ref[1:2, :] * p1 + w_ref[2:3, :] * u
    if b_ref is not None:
        y = y + b_ref[...]
    return y


def _ada_body(c_ref, w_ref, b_ref, o_ref):
    s = _silu(c_ref[...]).astype(BF16)
    o_ref[...] = _dot(s, w_ref[...].astype(BF16)) + b_ref[...]


def _ada_call(c_all, w_ada, b_ada):
    rows = c_all.shape[0]
    n_col = (N_MOD * D_MODEL) // ADA_COLS
    return pl.pallas_call(
        _ada_body,
        grid=(DEPTH, n_col),
        in_specs=[
            pl.BlockSpec((rows, D_MODEL), lambda l, n: (0, 0)),
            pl.BlockSpec((None, D_MODEL, ADA_COLS), lambda l, n: (l, 0, n)),
            pl.BlockSpec((None, 1, ADA_COLS), lambda l, n: (l, 0, n)),
        ],
        out_specs=pl.BlockSpec((None, rows, ADA_COLS), lambda l, n: (l, 0, n)),
        out_shape=jax.ShapeDtypeStruct((DEPTH, rows, N_MOD * D_MODEL), F32),
        compiler_params=pltpu.CompilerParams(
            dimension_semantics=("arbitrary", "arbitrary"),
            vmem_limit_bytes=VMEM_LIMIT_BYTES),
        name="ada_mod",
    )(c_all, w_ada, b_ada.reshape(DEPTH, 1, N_MOD * D_MODEL))


def _mixer_core(x, sh, sc, gt, gpre_ref, w_in_ref, g_v_ref, g_oa_ref, g_ob_ref, w_out_ref,
                gpost_ref, oa_s, spatial_fn, conv_fn):
    h = (_rms(x, gpre_ref[...]) * (1.0 + sc) + sh).astype(BF16)
    u = _gelu(_dot(h, w_in_ref[:, 0:W_A]))
    v = _rms(_gelu(_dot(h, w_in_ref[:, W_A:2 * W_A])), g_v_ref[...])
    spatial_fn(u, v)
    ra = _rms(oa_s[...], g_oa_ref[...]).astype(BF16)
    gate_b = _dot(h, w_in_ref[:, 2 * W_A:2 * W_A + W_B])
    gate_c = _dot(h, w_in_ref[:, 2 * W_A + W_B:2 * W_A + 2 * W_B])
    h_b = _dot(h, w_in_ref[:, 2 * W_A + 2 * W_B:])
    ci = gate_c * h_b
    rb = _rms(gate_b * conv_fn(ci), g_ob_ref[...]).astype(BF16)
    merged = _dot(ra, w_out_ref[0:W_A, :]) + _dot(rb, w_out_ref[W_A:, :])
    y = x + gt * _rms(merged, gpost_ref[...])
    return y, v, ci


def _mixer_prompt_body(tiles_per_seq, x_ref, sh_ref, sc_ref, gt_ref, gpre_ref, w_in_ref, g_v_ref,
                       wsp_ref, bsp_ref, wconv_ref, g_oa_ref, g_ob_ref, w_out_ref, gpost_ref,
                       y_ref, nmix_ref, vrows_ref, oa_s, carry_s):
    i = pl.program_id(0)
    tm = x_ref.shape[0]
    first = (i % tiles_per_seq) == 0

    def spatial_fn(u, v):
        vb = v.astype(BF16)
        tril = (lax.broadcasted_iota(jnp.int32, (CHUNK, CHUNK), 0)
                >= lax.broadcasted_iota(jnp.int32, (CHUNK, CHUNK), 1))
        for hd in range(N_HEADS_A):
            cols = slice(hd * HEAD_DIM_A, (hd + 1) * HEAD_DIM_A)
            wm = jnp.where(tril, wsp_ref[hd], 0.0).astype(BF16)
            bias = bsp_ref[:, hd:hd + 1]
            for c in range(tm // CHUNK):
                rows = slice(c * CHUNK, (c + 1) * CHUNK)
                mixed = _dot(wm, vb[rows, cols]) + bias
                oa_s[rows, cols] = u[rows, cols] * mixed

    def conv_fn(ci):
        prev2 = jnp.where(first, 0.0, carry_s[6:7, :])
        prev1 = jnp.where(first, 0.0, carry_s[7:8, :])
        return _conv_rows(ci, prev2, prev1, wconv_ref, None)

    y, v, ci = _mixer_core(x_ref[...], sh_ref[...], sc_ref[...], gt_ref[...], gpre_ref, w_in_ref,
                           g_v_ref, g_oa_ref, g_ob_ref, w_out_ref, gpost_ref, oa_s,
                           spatial_fn, conv_fn)
    y_ref[...] = y
    carry_s[...] = ci[tm - 8:, :]
    nmix_ref[...] = ci[tm - 2:, :]
    vrows_ref[...] = v[tm - CHUNK:, :]


def _mixer_sample_body(layer, x_ref, sh_ref, sc_ref, gt_ref, gpre_ref, w_in_ref, g_v_ref,
                       wsp_ref, bsp_ref, wconv_ref, past0_ref, past1_ref, g_oa_ref, g_ob_ref,
                       w_out_ref, gpost_ref, y_ref, nmix_ref, vrows_ref, oa_s):
    n_t, bb, _ = x_ref.shape
    tm = n_t * bb

    def rep(m_ref):
        return jnp.concatenate([m_ref[...]] * n_t, axis=0)

    def spatial_fn(u, v):
        for hd in range(N_HEADS_A):
            cols = slice(hd * HEAD_DIM_A, (hd + 1) * HEAD_DIM_A)
            base = (layer * N_HEADS_A + hd) * n_t
            for t in range(n_t):
                mixed = None
                for s in range(t + 1):
                    term = wsp_ref[(base + t) * n_t + s] * v[s * bb:(s + 1) * bb, cols]
                    mixed = term if mixed is None else mixed + term
                mixed = mixed + bsp_ref[base + t]
                oa_s[t * bb:(t + 1) * bb, cols] = u[t * bb:(t + 1) * bb, cols] * mixed

    def conv_fn(ci):
        return _conv_blocks(ci, past0_ref[...], past1_ref[...], wconv_ref, None, bb)

    x = x_ref[...].reshape(tm, D_MODEL)
    y, v, ci = _mixer_core(x, rep(sh_ref), rep(sc_ref), rep(gt_ref), gpre_ref, w_in_ref,
                           g_v_ref, g_oa_ref, g_ob_ref, w_out_ref, gpost_ref, oa_s,
                           spatial_fn, conv_fn)
    y_ref[...] = y.reshape(n_t, bb, D_MODEL)
    nmix_ref[...] = ci[tm - 2 * bb:, :].reshape(2, bb, W_B)
    vrows_ref[...] = v.reshape(n_t, bb, W_A)


def _layer_vec(arr, l):
    c = arr.shape[-1]
    return arr.reshape(DEPTH, 1, c), pl.BlockSpec((None, 1, c), lambda *_: (l, 0, 0))


def _resident(shape, index):
    return pl.BlockSpec(shape, lambda *_: index, pipeline_mode=pl.Buffered(1))


def _mixer_prompt_call(l, x, mod_p, g_pre, w_in_bf, g_v, w_spatial, b_spatial_t, w_conv, g_oa, g_ob,
                       w_out_bf, g_post):
    n_tok = x.shape[0]
    n_seq = mod_p.shape[1]
    seq = n_tok // n_seq
    tm = MIX_ROWS
    tps = seq // tm

    def mod_spec(k):
        return pl.BlockSpec((None, None, 1, D_MODEL), lambda i: (l, i // tps, 0, k))

    gpre_a, gpre_s = _layer_vec(g_pre, l)
    gv_a, gv_s = _layer_vec(g_v, l)
    goa_a, goa_s = _layer_vec(g_oa, l)
    gob_a, gob_s = _layer_vec(g_ob, l)
    gpost_a, gpost_s = _layer_vec(g_post, l)
    return pl.pallas_call(
        functools.partial(_mixer_prompt_body, tps),
        grid=(n_tok // tm,),
        in_specs=[
            pl.BlockSpec((tm, D_MODEL), lambda i: (i, 0)),
            mod_spec(0), mod_spec(1), mod_spec(2),
            gpre_s,
            _resident((None, D_MODEL, IN_COLS), (l, 0, 0)),
            gv_s,
            pl.BlockSpec((None, N_HEADS_A, CHUNK, CHUNK), lambda i: (l, 0, 0, 0)),
            pl.BlockSpec((None, CHUNK, N_HEADS_A), lambda i: (l, 0, 0)),
            pl.BlockSpec((None, 3, W_B), lambda i: (l, 0, 0)),
            goa_s, gob_s,
            _resident((None, D_MODEL, D_MODEL), (l, 0, 0)),
            gpost_s,
        ],
        out_specs=[
            pl.BlockSpec((tm, D_MODEL), lambda i: (i, 0)),
            pl.BlockSpec((None, 2, W_B), lambda i: (i // tps, 0, 0)),
            pl.BlockSpec((None, CHUNK, W_A), lambda i: (i // tps, 0, 0)),
        ],
        out_shape=[
            jax.ShapeDtypeStruct((n_tok, D_MODEL), F32),
            jax.ShapeDtypeStruct((n_seq, 2, W_B), F32),
            jax.ShapeDtypeStruct((n_seq, CHUNK, W_A), F32),
        ],
        scratch_shapes=[pltpu.VMEM((tm, W_A), F32), pltpu.VMEM((8, W_B), F32)],
        compiler_params=pltpu.CompilerParams(
            dimension_semantics=("arbitrary",), vmem_limit_bytes=VMEM_LIMIT_BYTES),
        name="mixer_prompt",
    )(x, mod_p, mod_p, mod_p, gpre_a, w_in_bf, gv_a, w_spatial, b_spatial_t, w_conv, goa_a, gob_a,
      w_out_bf, gpost_a)


def _mixer_sample_call(l, x, mod, state_mix, g_pre, w_in_bf, g_v, wsp_flat, bsp_flat, w_conv, g_oa,
                       g_ob, w_out_bf, g_post):
    n_t, n_b, _ = x.shape
    bb = MIX_ROWS // n_t
    tm = n_t * bb

    def mod_spec(k):
        return pl.BlockSpec((None, bb, D_MODEL), lambda i: (l, i, k))

    def past_spec(r):
        return pl.BlockSpec((bb, W_B), lambda i: (i, l * 2 + r))

    gpre_a, gpre_s = _layer_vec(g_pre, l)
    gv_a, gv_s = _layer_vec(g_v, l)
    goa_a, goa_s = _layer_vec(g_oa, l)
    gob_a, gob_s = _layer_vec(g_ob, l)
    gpost_a, gpost_s = _layer_vec(g_post, l)
    smem = pl.BlockSpec(memory_space=pltpu.SMEM)
    return pl.pallas_call(
        functools.partial(_mixer_sample_body, l),
        grid=(n_b // bb,),
        in_specs=[
            pl.BlockSpec((n_t, bb, D_MODEL), lambda i: (0, i, 0)),
            mod_spec(0), mod_spec(1), mod_spec(2),
            gpre_s,
            _resident((None, D_MODEL, IN_COLS), (l, 0, 0)),
            gv_s,
            smem, smem,
            pl.BlockSpec((None, 3, W_B), lambda i: (l, 0, 0)),
            past_spec(0), past_spec(1),
            goa_s, gob_s,
            _resident((None, D_MODEL, D_MODEL), (l, 0, 0)),
            gpost_s,
        ],
        out_specs=[
            pl.BlockSpec((n_t, bb, D_MODEL), lambda i: (0, i, 0)),
            pl.BlockSpec((2, bb, W_B), lambda i: (0, i, 0)),
            pl.BlockSpec((n_t, bb, W_A), lambda i: (0, i, 0)),
        ],
        out_shape=[
            jax.ShapeDtypeStruct((n_t, n_b, D_MODEL), F32),
            jax.ShapeDtypeStruct((2, n_b, W_B), F32),
            jax.ShapeDtypeStruct((n_t, n_b, W_A), F32),
        ],
        scratch_shapes=[pltpu.VMEM((tm, W_A), F32)],
        compiler_params=pltpu.CompilerParams(
            dimension_semantics=("arbitrary",), vmem_limit_bytes=VMEM_LIMIT_BYTES),
        name="mixer_sample",
    )(x, mod, mod, mod, gpre_a, w_in_bf, gv_a, wsp_flat, bsp_flat, w_conv, state_mix, state_mix,
      goa_a, gob_a, w_out_bf, gpost_a)


def _ffn_chunk_step(tm, halo, step, wg_ref, wv_ref, cwg_ref, cwv_ref, cbg_ref, cbv_ref, wd_ref,
                    h_s, upg_s, upv_s, act_s, acc_add):
    n_k = tm // FFN_ROW_BLOCK

    def up(k):
        rows = slice(k * FFN_ROW_BLOCK, (k + 1) * FFN_ROW_BLOCK)
        dst = slice(halo + k * FFN_ROW_BLOCK, halo + (k + 1) * FFN_ROW_BLOCK)
        h = h_s[rows, :]
        upg_s[dst, :] = _dot(h, wg_ref[...])
        upv_s[dst, :] = _dot(h, wv_ref[...])

    def conv(up_s, w_ref, b_ref, r, n):
        t2, t1, t0 = [up_s[pl.ds(halo - k * step + r, n), :] for k in (2, 1, 0)]
        return (w_ref[0:1, :] * t2 + w_ref[1:2, :] * t1 + w_ref[2:3, :] * t0) + b_ref[...]

    def gate_down(k):
        for r in range(k * FFN_ROW_BLOCK, (k + 1) * FFN_ROW_BLOCK, CONV_ROWS):
            cg = conv(upg_s, cwg_ref, cbg_ref, r, CONV_ROWS)
            cv = conv(upv_s, cwv_ref, cbv_ref, r, CONV_ROWS)
            act_s[r:r + CONV_ROWS, :] = (_silu(cg) * cv).astype(BF16)
        rows = slice(k * FFN_ROW_BLOCK, (k + 1) * FFN_ROW_BLOCK)
        acc_add(rows, _dot(act_s[rows, :], wd_ref[...]))

    up(0)
    for k in range(n_k):
        if k + 1 < n_k:
            up(k + 1)
        gate_down(k)


def _ffn_prompt_body(tiles_per_seq, x_ref, sh_ref, sc_ref, gt_ref, gpre_ref, wg_ref, wv_ref,
                     cwg_ref, cwv_ref, cbg_ref, cbv_ref, wd_ref, gpost_ref,
                     o_ref, nfg_ref, nfv_ref, h_s, upg_s, upv_s, act_s, carry_g, carry_v):
    i = pl.program_id(0)
    j = pl.program_id(1)
    n_j = pl.num_programs(1)
    tm = x_ref.shape[0]
    halo = 8
    first = (i % tiles_per_seq) == 0

    @pl.when(j == 0)
    def _():
        for r in range(0, tm, NORM_ROWS):
            rows = slice(r, r + NORM_ROWS)
            h_s[rows, :] = (_rms(x_ref[rows, :], gpre_ref[...]) * (1.0 + sc_ref[...])
                            + sh_ref[...]).astype(BF16)
        o_ref[...] = jnp.zeros_like(o_ref)

    upg_s[0:halo, :] = jnp.where(first, 0.0, carry_g[j])
    upv_s[0:halo, :] = jnp.where(first, 0.0, carry_v[j])

    def acc_add(rows, part):
        o_ref[rows, :] += part

    _ffn_chunk_step(tm, halo, 1, wg_ref, wv_ref, cwg_ref, cwv_ref, cbg_ref, cbv_ref, wd_ref,
                    h_s, upg_s, upv_s, act_s, acc_add)

    carry_g[j] = upg_s[tm:tm + halo, :]
    carry_v[j] = upv_s[tm:tm + halo, :]
    nfg_ref[...] = upg_s[halo + tm - 2:halo + tm, :]
    nfv_ref[...] = upv_s[halo + tm - 2:halo + tm, :]

    @pl.when(j == n_j - 1)
    def _():
        for r in range(0, tm, NORM_ROWS):
            rows = slice(r, r + NORM_ROWS)
            o_ref[rows, :] = x_ref[rows, :] + gt_ref[...] * _rms(o_ref[rows, :], gpost_ref[...])


def _ffn_sample_body(x_ref, sh_ref, sc_ref, gt_ref, gpre_ref, wg_ref, wv_ref,
                     cwg_ref, cwv_ref, cbg_ref, cbv_ref, pg0_ref, pg1_ref, pv0_ref, pv1_ref,
                     wd_ref, gpost_ref, o_ref, nfg_ref, nfv_ref, h_s, upg_s, upv_s, act_s, acc_s):
    j = pl.program_id(1)
    n_j = pl.num_programs(1)
    n_t, bb, _ = x_ref.shape
    tm = n_t * bb
    halo = 2 * bb

    @pl.when(j == 0)
    def _():
        for t in range(n_t):
            h_s[t * bb:(t + 1) * bb, :] = (_rms(x_ref[t], gpre_ref[...]) * (1.0 + sc_ref[...])
                                           + sh_ref[...]).astype(BF16)
        acc_s[...] = jnp.zeros_like(acc_s)

    upg_s[0:bb, :] = pg0_ref[...]
    upg_s[bb:halo, :] = pg1_ref[...]
    upv_s[0:bb, :] = pv0_ref[...]
    upv_s[bb:halo, :] = pv1_ref[...]

    def acc_add(rows, part):
        acc_s[rows, :] += part

    _ffn_chunk_step(tm, halo, bb, wg_ref, wv_ref, cwg_ref, cwv_ref, cbg_ref, cbv_ref, wd_ref,
                    h_s, upg_s, upv_s, act_s, acc_add)

    for k in range(2):
        rows = slice(tm + k * bb, tm + (k + 1) * bb)
        nfg_ref[k] = upg_s[rows, :]
        nfv_ref[k] = upv_s[rows, :]

    @pl.when(j == n_j - 1)
    def _():
        for t in range(n_t):
            o_ref[t] = x_ref[t] + gt_ref[...] * _rms(acc_s[t * bb:(t + 1) * bb, :], gpost_ref[...])


def _ffn_weight_specs(l, fc):
    n_j = D_FF // fc
    return [
        pl.BlockSpec((None, D_MODEL, fc), lambda i, j: (l, 0, j)),
        pl.BlockSpec((None, D_MODEL, fc), lambda i, j: (l, 0, n_j + j)),
        pl.BlockSpec((None, 3, fc), lambda i, j: (l, 0, j)),
        pl.BlockSpec((None, 3, fc), lambda i, j: (l, 0, n_j + j)),
        pl.BlockSpec((None, 1, fc), lambda i, j: (l, 0, j)),
        pl.BlockSpec((None, 1, fc), lambda i, j: (l, 0, n_j + j)),
    ]


def _ffn_prompt_call(l, x, mod_p, g_pre, w_up_bf, w_conv, b_conv, w_down_bf, g_post):
    n_tok = x.shape[0]
    n_seq = mod_p.shape[1]
    seq = n_tok // n_seq
    tm, fc = FFN_ROWS, FFN_COLS
    tps = seq // tm
    n_j = D_FF // fc
    halo = 8

    def mod_spec(k):
        return pl.BlockSpec((None, None, 1, D_MODEL), lambda i, j: (l, i // tps, 0, k))

    gpre_a, gpre_s = _layer_vec(g_pre, l)
    gpost_a, gpost_s = _layer_vec(g_post, l)
    b3 = b_conv.reshape(DEPTH, 1, 2 * D_FF)
    return pl.pallas_call(
        functools.partial(_ffn_prompt_body, tps),
        grid=(n_tok // tm, n_j),
        in_specs=[
            pl.BlockSpec((tm, D_MODEL), lambda i, j: (i, 0)),
            mod_spec(3), mod_spec(4), mod_spec(5),
            gpre_s,
            *_ffn_weight_specs(l, fc),
            pl.BlockSpec((None, fc, D_MODEL), lambda i, j: (l, j, 0)),
            gpost_s,
        ],
        out_specs=[
            pl.BlockSpec((tm, D_MODEL), lambda i, j: (i, 0)),
            pl.BlockSpec((None, 2, fc), lambda i, j: (i, 0, j)),
            pl.BlockSpec((None, 2, fc), lambda i, j: (i, 0, j)),
        ],
        out_shape=[
            jax.ShapeDtypeStruct((n_tok, D_MODEL), F32),
            jax.ShapeDtypeStruct((n_tok // tm, 2, D_FF), F32),
            jax.ShapeDtypeStruct((n_tok // tm, 2, D_FF), F32),
        ],
        scratch_shapes=[
            pltpu.VMEM((tm, D_MODEL), BF16),
            pltpu.VMEM((halo + tm, fc), F32),
            pltpu.VMEM((halo + tm, fc), F32),
            pltpu.VMEM((tm, fc), BF16),
            pltpu.VMEM((n_j, halo, fc), F32),
            pltpu.VMEM((n_j, halo, fc), F32),
        ],
        compiler_params=pltpu.CompilerParams(
            dimension_semantics=("arbitrary", "arbitrary"), vmem_limit_bytes=VMEM_LIMIT_BYTES),
        name="ffn_prompt",
    )(x, mod_p, mod_p, mod_p, gpre_a, w_up_bf, w_up_bf, w_conv, w_conv, b3, b3, w_down_bf, gpost_a)


def _ffn_sample_call(l, x, mod, state_ffn, g_pre, w_up_bf, w_conv, b_conv, w_down_bf, g_post):
    n_t, n_b, _ = x.shape
    fc = FFN_COLS
    bb = FFN_ROWS // n_t
    tm = n_t * bb
    n_j = D_FF // fc

    def mod_spec(k):
        return pl.BlockSpec((None, bb, D_MODEL), lambda i, j: (l, i, k))

    def past_spec(r, half):
        base = ((l * 2 + r) * 2 + half) * n_j
        return pl.BlockSpec((bb, fc), lambda i, j: (i, base + j))

    gpre_a, gpre_s = _layer_vec(g_pre, l)
    gpost_a, gpost_s = _layer_vec(g_post, l)
    b3 = b_conv.reshape(DEPTH, 1, 2 * D_FF)
    return pl.pallas_call(
        _ffn_sample_body,
        grid=(n_b // bb, n_j),
        in_specs=[
            pl.BlockSpec((n_t, bb, D_MODEL), lambda i, j: (0, i, 0)),
            mod_spec(3), mod_spec(4), mod_spec(5),
            gpre_s,
            *_ffn_weight_specs(l, fc),
            past_spec(0, 0), past_spec(1, 0), past_spec(0, 1), past_spec(1, 1),
            pl.BlockSpec((None, fc, D_MODEL), lambda i, j: (l, j, 0)),
            gpost_s,
        ],
        out_specs=[
            pl.BlockSpec((n_t, bb, D_MODEL), lambda i, j: (0, i, 0)),
            pl.BlockSpec((2, bb, fc), lambda i, j: (0, i, j)),
            pl.BlockSpec((2, bb, fc), lambda i, j: (0, i, j)),
        ],
        out_shape=[
            jax.ShapeDtypeStruct((n_t, n_b, D_MODEL), F32),
            jax.ShapeDtypeStruct((2, n_b, D_FF), F32),
            jax.ShapeDtypeStruct((2, n_b, D_FF), F32),
        ],
        scratch_shapes=[
            pltpu.VMEM((tm, D_MODEL), BF16),
            pltpu.VMEM((2 * bb + tm, fc), F32),
            pltpu.VMEM((2 * bb + tm, fc), F32),
            pltpu.VMEM((tm, fc), BF16),
            pltpu.VMEM((tm, D_MODEL), F32),
        ],
        compiler_params=pltpu.CompilerParams(
            dimension_semantics=("arbitrary", "arbitrary"), vmem_limit_bytes=VMEM_LIMIT_BYTES),
        name="ffn_sample",
    )(x, mod, mod, mod, gpre_a, w_up_bf, w_up_bf, w_conv, w_conv, b3, b3,
      state_ffn, state_ffn, state_ffn, state_ffn, w_down_bf, gpost_a)


def kernel(x_prompt, x_sample, state_conv_mix, state_conv_ffn, c_prompt, c_sample, w_ada, b_ada, g_pre_mix, g_post_mix, g_pre_ffn, g_post_ffn, w_in, g_v, w_spatial, b_spatial, w_conv_mix, g_out_a, g_out_b, w_out, w_up, w_conv_ffn, b_conv_ffn, w_down):
    n_seq, seq, _ = x_prompt.shape
    n_b, n_t, _ = x_sample.shape

    pad = (-(n_b + n_seq)) % 8
    c_all = jnp.concatenate([c_sample, c_prompt, jnp.zeros((pad, D_MODEL), F32)], axis=0)
    mod = _ada_call(c_all, w_ada, b_ada)
    mod_p = mod[:, n_b:n_b + n_seq].reshape(DEPTH, n_seq, 1, N_MOD * D_MODEL)

    w_in_bf = w_in.astype(BF16)
    w_out_bf = w_out.astype(BF16)
    w_up_bf = w_up.astype(BF16)
    w_down_bf = w_down.astype(BF16)
    b_spatial_t = jnp.swapaxes(b_spatial, 1, 2)
    wsp_flat = w_spatial[:, :, :n_t, :n_t].reshape(-1)
    bsp_flat = b_spatial[:, :, :n_t].reshape(-1)
    state_mix = state_conv_mix.reshape(n_b, DEPTH * 2 * W_B)
    state_ffn = state_conv_ffn.reshape(n_b, DEPTH * 2 * 2 * D_FF)

    xp = x_prompt.reshape(n_seq * seq, D_MODEL)
    xs = jnp.swapaxes(x_sample, 0, 1)
    mix_p, mix_s, ffn_p, ffn_s, vr_p, vr_s = [], [], [], [], [], []
    for l in range(DEPTH):
        xp, nm, vr = _mixer_prompt_call(l, xp, mod_p, g_pre_mix, w_in_bf, g_v, w_spatial,
                                        b_spatial_t, w_conv_mix, g_out_a, g_out_b, w_out_bf,
                                        g_post_mix)
        mix_p.append(nm)
        vr_p.append(vr)
        xs, nm, vr = _mixer_sample_call(l, xs, mod, state_mix, g_pre_mix, w_in_bf, g_v, wsp_flat,
                                        bsp_flat, w_conv_mix, g_out_a, g_out_b, w_out_bf,
                                        g_post_mix)
        mix_s.append(jnp.swapaxes(nm, 0, 1))
        vr_s.append(jnp.swapaxes(vr, 0, 1))
        xp, nfg, nfv = _ffn_prompt_call(l, xp, mod_p, g_pre_ffn, w_up_bf, w_conv_ffn, b_conv_ffn,
                                        w_down_bf, g_post_ffn)
        tps = nfg.shape[0] // n_seq
        ffn_p.append(jnp.concatenate([nfg, nfv], axis=-1)[tps - 1::tps])
        xs, nfg, nfv = _ffn_sample_call(l, xs, mod, state_ffn, g_pre_ffn, w_up_bf, w_conv_ffn,
                                        b_conv_ffn, w_down_bf, g_post_ffn)
        ffn_s.append(jnp.swapaxes(jnp.concatenate([nfg, nfv], axis=-1), 0, 1))

    return (xp.reshape(n_seq, seq, D_MODEL), jnp.swapaxes(xs, 0, 1),
            jnp.stack(mix_p, axis=1), jnp.stack(mix_s, axis=1),
            jnp.stack(ffn_p, axis=1), jnp.stack(ffn_s, axis=1),
            jnp.stack(vr_p, axis=1), jnp.stack(vr_s, axis=1))
```

```python
import functools

import jax
import jax.numpy as jnp
from jax import lax
from jax.experimental import pallas as pl
from jax.experimental.pallas import tpu as pltpu

F32 = jnp.float32
BF16 = jnp.bfloat16

D_MODEL = 2048
DEPTH = 4
CHUNK = 128
N_HEADS_A = 8
HEAD_DIM_A = 128
W_A = N_HEADS_A * HEAD_DIM_A
W_B = D_MODEL - W_A
D_FF = 5632
N_MOD = 6
IN_COLS = 2 * W_A + 3 * W_B
EPS = 1e-6

MIX_ROWS = 256
FFN_ROWS = 512
FFN_COLS = 512
NORM_ROWS = 32
CONV_ROWS = 64
FFN_ROW_BLOCK = 128
FFN_UP_AHEAD = 2
DOWN_COLS = 512
CAST_IN_BLOCKS = 10
CAST_OUT_BLOCKS = 8
ADA_COLS = 1024
VMEM_LIMIT_BYTES = 56 * 1024 * 1024


def _rms(x, g):
    ms = jnp.mean(x * x, axis=-1, keepdims=True)
    return x * lax.rsqrt(ms + EPS) * g


def _gelu(x):
    c = 0.7978845608028654
    return 0.5 * x * (1.0 + jnp.tanh(c * (x + 0.044715 * (x * x * x))))


def _silu(x):
    h = 0.5 * x
    return h + h * jnp.tanh(h)


def _dot(a, b):
    return jnp.dot(a, b, preferred_element_type=F32)


def _conv_rows(u, prev2, prev1, w_ref, b_ref):
    rows = lax.broadcasted_iota(jnp.int32, u.shape, 0)
    p1 = jnp.where(rows == 0, prev1, pltpu.roll(u, 1, 0))
    p2 = jnp.where(rows == 0, prev2, jnp.where(rows == 1, prev1, pltpu.roll(u, 2, 0)))
    y = w_ref[0:1, :] * p2 + w_ref[1:2, :] * p1 + w_ref[2:3, :] * u
    if b_ref is not None:
        y = y + b_ref[...]
    return y


def _conv_blocks(u, past0, past1, w_ref, b_ref, bb):
    t = u.shape[0]
    p1 = jnp.concatenate([past1, u[: t - bb]], axis=0)
    p2 = jnp.concatenate([past0, past1, u[: t - 2 * bb]], axis=0)
    y = w_ref[0:1, :] * p2 + w_ref[1:2, :] * p1 + w_ref[2:3, :] * u
    if b_ref is not None:
        y = y + b_ref[...]
    return y


def _ada_body(c_ref, w_ref, b_ref, o_ref):
    s = _silu(c_ref[...]).astype(BF16)
    o_ref[...] = _dot(s, w_ref[...].astype(BF16)) + b_ref[...]


def _ada_call(c_all, w_ada, b_ada):
    rows = c_all.shape[0]
    n_col = (N_MOD * D_MODEL) // ADA_COLS
    return pl.pallas_call(
        _ada_body,
        grid=(DEPTH, n_col),
        in_specs=[
            pl.BlockSpec((rows, D_MODEL), lambda l, n: (0, 0)),
            pl.BlockSpec((None, D_MODEL, ADA_COLS), lambda l, n: (l, 0, n)),
            pl.BlockSpec((None, 1, ADA_COLS), lambda l, n: (l, 0, n)),
        ],
        out_specs=pl.BlockSpec((None, rows, ADA_COLS), lambda l, n: (l, 0, n)),
        out_shape=jax.ShapeDtypeStruct((DEPTH, rows, N_MOD * D_MODEL), F32),
        compiler_params=pltpu.CompilerParams(
            dimension_semantics=("arbitrary", "arbitrary"),
            vmem_limit_bytes=VMEM_LIMIT_BYTES),
        name="ada_mod",
    )(c_all, w_ada, b_ada.reshape(DEPTH, 1, N_MOD * D_MODEL))


def _mixer_core(x, sh, sc, gt, gpre_ref, w_in_ref, g_v_ref, g_oa_ref, g_ob_ref, w_out_ref,
                gpost_ref, oa_s, spatial_fn, conv_fn):
    h = (_rms(x, gpre_ref[...]) * (1.0 + sc) + sh).astype(BF16)
    u = _gelu(_dot(h, w_in_ref[:, 0:W_A]))
    v = _rms(_gelu(_dot(h, w_in_ref[:, W_A:2 * W_A])), g_v_ref[...])
    spatial_fn(u, v)
    ra = _rms(oa_s[...], g_oa_ref[...]).astype(BF16)
    gate_b = _dot(h, w_in_ref[:, 2 * W_A:2 * W_A + W_B])
    gate_c = _dot(h, w_in_ref[:, 2 * W_A + W_B:2 * W_A + 2 * W_B])
    h_b = _dot(h, w_in_ref[:, 2 * W_A + 2 * W_B:])
    ci = gate_c * h_b
    rb = _rms(gate_b * conv_fn(ci), g_ob_ref[...]).astype(BF16)
    merged = _dot(ra, w_out_ref[0:W_A, :]) + _dot(rb, w_out_ref[W_A:, :])
    y = x + gt * _rms(merged, gpost_ref[...])
    return y, v, ci


def _mixer_prompt_body(tiles_per_seq, x_ref, sh_ref, sc_ref, gt_ref, gpre_ref, w_in_ref, g_v_ref,
                       wsp_ref, bsp_ref, wconv_ref, g_oa_ref, g_ob_ref, w_out_ref, gpost_ref,
                       y_ref, nmix_ref, vrows_ref, oa_s, carry_s):
    i = pl.program_id(0)
    tm = x_ref.shape[0]
    first = (i % tiles_per_seq) == 0

    def spatial_fn(u, v):
        vb = v.astype(BF16)
        tril = (lax.broadcasted_iota(jnp.int32, (CHUNK, CHUNK), 0)
                >= lax.broadcasted_iota(jnp.int32, (CHUNK, CHUNK), 1))
        for hd in range(N_HEADS_A):
            cols = slice(hd * HEAD_DIM_A, (hd + 1) * HEAD_DIM_A)
            wm = jnp.where(tril, wsp_ref[hd], 0.0).astype(BF16)
            bias = bsp_ref[:, hd:hd + 1]
            for c in range(tm // CHUNK):
                rows = slice(c * CHUNK, (c + 1) * CHUNK)
                mixed = _dot(wm, vb[rows, cols]) + bias
                oa_s[rows, cols] = u[rows, cols] * mixed

    def conv_fn(ci):
        prev2 = jnp.where(first, 0.0, carry_s[6:7, :])
        prev1 = jnp.where(first, 0.0, carry_s[7:8, :])
        return _conv_rows(ci, prev2, prev1, wconv_ref, None)

    y, v, ci = _mixer_core(x_ref[...], sh_ref[...], sc_ref[...], gt_ref[...], gpre_ref, w_in_ref,
                           g_v_ref, g_oa_ref, g_ob_ref, w_out_ref, gpost_ref, oa_s,
                           spatial_fn, conv_fn)
    y_ref[...] = y
    carry_s[...] = ci[tm - 8:, :]
    nmix_ref[...] = ci[tm - 2:, :]
    vrows_ref[...] = v[tm - CHUNK:, :]


def _mixer_sample_body(layer, x_ref, sh_ref, sc_ref, gt_ref, gpre_ref, w_in_ref, g_v_ref,
                       wsp_ref, bsp_ref, wconv_ref, past0_ref, past1_ref, g_oa_ref, g_ob_ref,
                       w_out_ref, gpost_ref, y_ref, nmix_ref, vrows_ref, oa_s):
    n_t, bb, _ = x_ref.shape
    tm = n_t * bb

    def rep(m_ref):
        return jnp.concatenate([m_ref[...]] * n_t, axis=0)

    def spatial_fn(u, v):
        for hd in range(N_HEADS_A):
            cols = slice(hd * HEAD_DIM_A, (hd + 1) * HEAD_DIM_A)
            base = (layer * N_HEADS_A + hd) * n_t
            for t in range(n_t):
                mixed = None
                for s in range(t + 1):
                    term = wsp_ref[(base + t) * n_t + s] * v[s * bb:(s + 1) * bb, cols]
                    mixed = term if mixed is None else mixed + term
                mixed = mixed + bsp_ref[base + t]
                oa_s[t * bb:(t + 1) * bb, cols] = u[t * bb:(t + 1) * bb, cols] * mixed

    def conv_fn(ci):
        return _conv_blocks(ci, past0_ref[...], past1_ref[...], wconv_ref, None, bb)

    x = x_ref[...].reshape(tm, D_MODEL)
    y, v, ci = _mixer_core(x, rep(sh_ref), rep(sc_ref), rep(gt_ref), gpre_ref, w_in_ref,
                           g_v_ref, g_oa_ref, g_ob_ref, w_out_ref, gpost_ref, oa_s,
                           spatial_fn, conv_fn)
    y_ref[...] = y.reshape(n_t, bb, D_MODEL)
    nmix_ref[...] = ci[tm - 2 * bb:, :].reshape(2, bb, W_B)
    vrows_ref[...] = v.reshape(n_t, bb, W_A)


def _layer_vec(arr, l):
    c = arr.shape[-1]
    return arr.reshape(DEPTH, 1, c), pl.BlockSpec((None, 1, c), lambda *_: (l, 0, 0))


def _resident(shape, index):
    return pl.BlockSpec(shape, lambda *_: index, pipeline_mode=pl.Buffered(1))


def _mixer_prompt_call(l, x, mod_p, g_pre, w_in_bf, g_v, w_spatial, b_spatial_t, w_conv, g_oa, g_ob,
                       w_out_bf, g_post):
    n_tok = x.shape[0]
    n_seq = mod_p.shape[1]
    seq = n_tok // n_seq
    tm = MIX_ROWS
    tps = seq // tm

    def mod_spec(k):
        return pl.BlockSpec((None, None, 1, D_MODEL), lambda i: (l, i // tps, 0, k))

    gpre_a, gpre_s = _layer_vec(g_pre, l)
    gv_a, gv_s = _layer_vec(g_v, l)
    goa_a, goa_s = _layer_vec(g_oa, l)
    gob_a, gob_s = _layer_vec(g_ob, l)
    gpost_a, gpost_s = _layer_vec(g_post, l)
    return pl.pallas_call(
        functools.partial(_mixer_prompt_body, tps),
        grid=(n_tok // tm,),
        in_specs=[
            pl.BlockSpec((tm, D_MODEL), lambda i: (i, 0)),
            mod_spec(0), mod_spec(1), mod_spec(2),
            gpre_s,
            _resident((D_MODEL, IN_COLS), (0, 0)),
            gv_s,
            pl.BlockSpec((None, N_HEADS_A, CHUNK, CHUNK), lambda i: (l, 0, 0, 0)),
            pl.BlockSpec((None, CHUNK, N_HEADS_A), lambda i: (l, 0, 0)),
            pl.BlockSpec((None, 3, W_B), lambda i: (l, 0, 0)),
            goa_s, gob_s,
            _resident((D_MODEL, D_MODEL), (0, 0)),
            gpost_s,
        ],
        out_specs=[
            pl.BlockSpec((tm, D_MODEL), lambda i: (i, 0)),
            pl.BlockSpec((None, 2, W_B), lambda i: (i // tps, 0, 0)),
            pl.BlockSpec((None, CHUNK, W_A), lambda i: (i // tps, 0, 0)),
        ],
        out_shape=[
            jax.ShapeDtypeStruct((n_tok, D_MODEL), F32),
            jax.ShapeDtypeStruct((n_seq, 2, W_B), F32),
            jax.ShapeDtypeStruct((n_seq, CHUNK, W_A), F32),
        ],
        scratch_shapes=[pltpu.VMEM((tm, W_A), F32), pltpu.VMEM((8, W_B), F32)],
        compiler_params=pltpu.CompilerParams(
            dimension_semantics=("arbitrary",), vmem_limit_bytes=VMEM_LIMIT_BYTES),
        name="mixer_prompt",
    )(x, mod_p, mod_p, mod_p, gpre_a, w_in_bf, gv_a, w_spatial, b_spatial_t, w_conv, goa_a, gob_a,
      w_out_bf, gpost_a)


def _mixer_sample_call(l, x, mod, state_mix, g_pre, w_in_bf, g_v, wsp_flat, bsp_flat, w_conv, g_oa,
                       g_ob, w_out_bf, g_post):
    n_t, n_b, _ = x.shape
    bb = MIX_ROWS // n_t
    tm = n_t * bb

    def mod_spec(k):
        return pl.BlockSpec((None, bb, D_MODEL), lambda i: (l, i, k))

    def past_spec(r):
        return pl.BlockSpec((bb, W_B), lambda i: (i, l * 2 + r))

    gpre_a, gpre_s = _layer_vec(g_pre, l)
    gv_a, gv_s = _layer_vec(g_v, l)
    goa_a, goa_s = _layer_vec(g_oa, l)
    gob_a, gob_s = _layer_vec(g_ob, l)
    gpost_a, gpost_s = _layer_vec(g_post, l)
    smem = pl.BlockSpec(memory_space=pltpu.SMEM)
    return pl.pallas_call(
        functools.partial(_mixer_sample_body, l),
        grid=(n_b // bb,),
        in_specs=[
            pl.BlockSpec((n_t, bb, D_MODEL), lambda i: (0, i, 0)),
            mod_spec(0), mod_spec(1), mod_spec(2),
            gpre_s,
            _resident((D_MODEL, IN_COLS), (0, 0)),
            gv_s,
            smem, smem,
            pl.BlockSpec((None, 3, W_B), lambda i: (l, 0, 0)),
            past_spec(0), past_spec(1),
            goa_s, gob_s,
            _resident((D_MODEL, D_MODEL), (0, 0)),
            gpost_s,
        ],
        out_specs=[
            pl.BlockSpec((n_t, bb, D_MODEL), lambda i: (0, i, 0)),
            pl.BlockSpec((2, bb, W_B), lambda i: (0, i, 0)),
            pl.BlockSpec((n_t, bb, W_A), lambda i: (0, i, 0)),
        ],
        out_shape=[
            jax.ShapeDtypeStruct((n_t, n_b, D_MODEL), F32),
            jax.ShapeDtypeStruct((2, n_b, W_B), F32),
            jax.ShapeDtypeStruct((n_t, n_b, W_A), F32),
        ],
        scratch_shapes=[pltpu.VMEM((tm, W_A), F32)],
        compiler_params=pltpu.CompilerParams(
            dimension_semantics=("arbitrary",), vmem_limit_bytes=VMEM_LIMIT_BYTES),
        name="mixer_sample",
    )(x, mod, mod, mod, gpre_a, w_in_bf, gv_a, wsp_flat, bsp_flat, w_conv, state_mix, state_mix,
      goa_a, gob_a, w_out_bf, gpost_a)


def _ffn_chunk_step(tm, halo, step, wg_ref, wv_ref, cwg_ref, cwv_ref, cbg_ref, cbv_ref, wd_ref,
                    h_s, upg_s, upv_s, act_s, acc_add, before_up=None, after_down=None):
    n_k = tm // FFN_ROW_BLOCK

    def up(k):
        if before_up is not None:
            before_up(k)
        rows = slice(k * FFN_ROW_BLOCK, (k + 1) * FFN_ROW_BLOCK)
        dst = slice(halo + k * FFN_ROW_BLOCK, halo + (k + 1) * FFN_ROW_BLOCK)
        h = h_s[rows, :]
        upg_s[dst, :] = _dot(h, wg_ref[...])
        upv_s[dst, :] = _dot(h, wv_ref[...])

    def conv(up_s, w_ref, b_ref, r, n):
        t2, t1, t0 = [up_s[pl.ds(halo - k * step + r, n), :] for k in (2, 1, 0)]
        return (w_ref[0:1, :] * t2 + w_ref[1:2, :] * t1 + w_ref[2:3, :] * t0) + b_ref[...]

    def gate_down(k):
        for r in range(k * FFN_ROW_BLOCK, (k + 1) * FFN_ROW_BLOCK, CONV_ROWS):
            cg = conv(upg_s, cwg_ref, cbg_ref, r, CONV_ROWS)
            cv = conv(upv_s, cwv_ref, cbv_ref, r, CONV_ROWS)
            act_s[r:r + CONV_ROWS, :] = (_silu(cg) * cv).astype(BF16)
        rows = slice(k * FFN_ROW_BLOCK, (k + 1) * FFN_ROW_BLOCK)
        for c in range(0, D_MODEL, DOWN_COLS):
            acc_add(k, c, _dot(act_s[rows, :], wd_ref[:, c:c + DOWN_COLS]))
        if after_down is not None:
            after_down(k)

    for k in range(min(FFN_UP_AHEAD, n_k)):
        up(k)
    for k in range(n_k):
        if k + FFN_UP_AHEAD < n_k:
            up(k + FFN_UP_AHEAD)
        gate_down(k)


def _ffn_step_variants(j, n_j, step):
    @pl.when(j == 0)
    def _():
        step(True, False)

    @pl.when(jnp.logical_and(j > 0, j < n_j - 1))
    def _():
        step(False, False)

    @pl.when(j == n_j - 1)
    def _():
        step(False, True)


def _ffn_prompt_body(tiles_per_seq, n_cast, x_ref, sh_ref, sc_ref, gt_ref, gpre_ref, wg_ref, wv_ref,
                     cwg_ref, cwv_ref, cbg_ref, cbv_ref, wd_ref, gpost_ref, *rest):
    cast_in, rest = rest[:n_cast], rest[n_cast:]
    o_ref, nfg_ref, nfv_ref = rest[:3]
    cast_out, rest = rest[3:3 + n_cast], rest[3 + n_cast:]
    h_s, upg_s, upv_s, act_s, carry_g, carry_v = rest
    i = pl.program_id(0)
    j = pl.program_id(1)
    n_j = pl.num_programs(1)
    tm = x_ref.shape[0]
    halo = 8
    first_tile = (i % tiles_per_seq) == 0

    for src, dst in zip(cast_in, cast_out):
        dst[...] = src[...].astype(BF16)

    def norm_in(k):
        for r in range(k * FFN_ROW_BLOCK, (k + 1) * FFN_ROW_BLOCK, NORM_ROWS):
            rows = slice(r, r + NORM_ROWS)
            h_s[rows, :] = (_rms(x_ref[rows, :], gpre_ref[...]) * (1.0 + sc_ref[...])
                            + sh_ref[...]).astype(BF16)

    def norm_out(k):
        for r in range(k * FFN_ROW_BLOCK, (k + 1) * FFN_ROW_BLOCK, NORM_ROWS):
            rows = slice(r, r + NORM_ROWS)
            o_ref[rows, :] = x_ref[rows, :] + gt_ref[...] * _rms(o_ref[rows, :], gpost_ref[...])

    def step(first, last):
        upg_s[0:halo, :] = jnp.where(first_tile, 0.0, carry_g[j])
        upv_s[0:halo, :] = jnp.where(first_tile, 0.0, carry_v[j])

        def acc_add(k, c, part):
            rows = slice(k * FFN_ROW_BLOCK, (k + 1) * FFN_ROW_BLOCK)
            cols = slice(c, c + part.shape[1])
            if first:
                o_ref[rows, cols] = part
            else:
                o_ref[rows, cols] += part

        _ffn_chunk_step(tm, halo, 1, wg_ref, wv_ref, cwg_ref, cwv_ref, cbg_ref, cbv_ref, wd_ref,
                        h_s, upg_s, upv_s, act_s, acc_add,
                        before_up=norm_in if first else None,
                        after_down=norm_out if last else None)
        carry_g[j] = upg_s[tm:tm + halo, :]
        carry_v[j] = upv_s[tm:tm + halo, :]
        nfg_ref[...] = upg_s[halo + tm - 2:halo + tm, :]
        nfv_ref[...] = upv_s[halo + tm - 2:halo + tm, :]

    _ffn_step_variants(j, n_j, step)


def _ffn_sample_body(x_ref, sh_ref, sc_ref, gt_ref, gpre_ref, wg_ref, wv_ref,
                     cwg_ref, cwv_ref, cbg_ref, cbv_ref, pg_ref, pv_ref,
                     wd_ref, gpost_ref, o_ref, nfg_ref, nfv_ref, h_s, upg_s, upv_s, act_s, acc_s):
    j = pl.program_id(1)
    n_j = pl.num_programs(1)
    n_t, bb, _ = x_ref.shape
    tm = n_t * bb
    halo = 2 * bb
    t_per_block = FFN_ROW_BLOCK // bb

    def passes(k):
        for t in range(k * t_per_block, (k + 1) * t_per_block):
            for r in range(0, bb, NORM_ROWS):
                yield t, slice(r, r + NORM_ROWS), slice(t * bb + r, t * bb + r + NORM_ROWS)

    def norm_in(k):
        for t, rb, rows in passes(k):
            h_s[rows, :] = (_rms(x_ref[t, rb, :], gpre_ref[...]) * (1.0 + sc_ref[rb, :])
                            + sh_ref[rb, :]).astype(BF16)

    def norm_out(k):
        for t, rb, rows in passes(k):
            o_ref[t, rb, :] = x_ref[t, rb, :] + gt_ref[rb, :] * _rms(acc_s[rows, :], gpost_ref[...])

    def step(first, last):
        for r in range(2):
            upg_s[r * bb:(r + 1) * bb, :] = pg_ref[:, r, :]
            upv_s[r * bb:(r + 1) * bb, :] = pv_ref[:, r, :]

        def acc_add(k, c, part):
            rows = slice(k * FFN_ROW_BLOCK, (k + 1) * FFN_ROW_BLOCK)
            cols = slice(c, c + part.shape[1])
            if first:
                acc_s[rows, cols] = part
            else:
                acc_s[rows, cols] += part

        _ffn_chunk_step(tm, halo, bb, wg_ref, wv_ref, cwg_ref, cwv_ref, cbg_ref, cbv_ref, wd_ref,
                        h_s, upg_s, upv_s, act_s, acc_add,
                        before_up=norm_in if first else None,
                        after_down=norm_out if last else None)
        for k in range(2):
            rows = slice(tm + k * bb, tm + (k + 1) * bb)
            nfg_ref[k] = upg_s[rows, :]
            nfv_ref[k] = upv_s[rows, :]

    _ffn_step_variants(j, n_j, step)


def _ffn_weight_specs(l, fc):
    n_j = D_FF // fc
    return [
        pl.BlockSpec((D_MODEL, fc), lambda i, j: (0, j)),
        pl.BlockSpec((D_MODEL, fc), lambda i, j: (0, n_j + j)),
        pl.BlockSpec((None, 3, fc), lambda i, j: (l, 0, j)),
        pl.BlockSpec((None, 3, fc), lambda i, j: (l, 0, n_j + j)),
        pl.BlockSpec((None, 1, fc), lambda i, j: (l, 0, j)),
        pl.BlockSpec((None, 1, fc), lambda i, j: (l, 0, n_j + j)),
    ]


def _next_layer_cast_specs(l, n_tiles, n_j, weights):
    w_up, w_down, w_in, w_out = weights
    rows = D_MODEL // n_tiles
    in_blocks, out_blocks = CAST_IN_BLOCKS, CAST_OUT_BLOCKS
    assert in_blocks <= n_j and out_blocks <= n_j
    shapes = [
        (w_up, (rows, 2 * D_FF // n_j), lambda i, j: (i, j)),
        (w_down, (D_FF // n_j, rows), lambda i, j: (j, i)),
        (w_in, (rows, IN_COLS // in_blocks), lambda i, j: (i, jnp.minimum(j, in_blocks - 1))),
        (w_out, (rows, D_MODEL // out_blocks), lambda i, j: (i, jnp.minimum(j, out_blocks - 1))),
    ]
    in_specs, out_specs, out_shape = [], [], []
    for w, blk, idx in shapes:
        in_specs.append(pl.BlockSpec((None,) + blk, lambda i, j, idx=idx: (l + 1,) + idx(i, j)))
        out_specs.append(pl.BlockSpec(blk, idx))
        out_shape.append(jax.ShapeDtypeStruct(w.shape[1:], BF16))
    return in_specs, out_specs, out_shape


def _ffn_prompt_call(l, x, mod_p, g_pre, w_up_bf, w_conv, b_conv, w_down_bf, g_post, next_weights):
    n_tok = x.shape[0]
    n_seq = mod_p.shape[1]
    seq = n_tok // n_seq
    tm, fc = FFN_ROWS, FFN_COLS
    tps = seq // tm
    n_j = D_FF // fc
    halo = 8
    cast_in, cast_out, cast_shape = ([], [], [])
    if next_weights is not None:
        cast_in, cast_out, cast_shape = _next_layer_cast_specs(l, n_tok // tm, n_j, next_weights)

    def mod_spec(k):
        return pl.BlockSpec((None, None, 1, D_MODEL), lambda i, j: (l, i // tps, 0, k))

    gpre_a, gpre_s = _layer_vec(g_pre, l)
    gpost_a, gpost_s = _layer_vec(g_post, l)
    b3 = b_conv.reshape(DEPTH, 1, 2 * D_FF)
    outs = pl.pallas_call(
        functools.partial(_ffn_prompt_body, tps, len(cast_in)),
        grid=(n_tok // tm, n_j),
        in_specs=[
            pl.BlockSpec((tm, D_MODEL), lambda i, j: (i, 0)),
            mod_spec(3), mod_spec(4), mod_spec(5),
            gpre_s,
            *_ffn_weight_specs(l, fc),
            pl.BlockSpec((fc, D_MODEL), lambda i, j: (j, 0)),
            gpost_s,
            *cast_in,
        ],
        out_specs=[
            pl.BlockSpec((tm, D_MODEL), lambda i, j: (i, 0)),
            pl.BlockSpec((None, 2, fc), lambda i, j: (i, 0, j)),
            pl.BlockSpec((None, 2, fc), lambda i, j: (i, 0, j)),
            *cast_out,
        ],
        out_shape=[
            jax.ShapeDtypeStruct((n_tok, D_MODEL), F32),
            jax.ShapeDtypeStruct((n_tok // tm, 2, D_FF), F32),
            jax.ShapeDtypeStruct((n_tok // tm, 2, D_FF), F32),
            *cast_shape,
        ],
        scratch_shapes=[
            pltpu.VMEM((tm, D_MODEL), BF16),
            pltpu.VMEM((halo + tm, fc), F32),
            pltpu.VMEM((halo + tm, fc), F32),
            pltpu.VMEM((tm, fc), BF16),
            pltpu.VMEM((n_j, halo, fc), F32),
            pltpu.VMEM((n_j, halo, fc), F32),
        ],
        compiler_params=pltpu.CompilerParams(
            dimension_semantics=("arbitrary", "arbitrary"), vmem_limit_bytes=VMEM_LIMIT_BYTES),
        name="ffn_prompt",
    )(x, mod_p, mod_p, mod_p, gpre_a, w_up_bf, w_up_bf, w_conv, w_conv, b3, b3, w_down_bf, gpost_a,
      *(next_weights or ()))
    return outs[0], outs[1], outs[2], tuple(outs[3:])


def _ffn_sample_call(l, x, mod, state_ffn, g_pre, w_up_bf, w_conv, b_conv, w_down_bf, g_post):
    n_t, n_b, _ = x.shape
    fc = FFN_COLS
    bb = FFN_ROWS // n_t
    tm = n_t * bb
    n_j = D_FF // fc

    def mod_spec(k):
        return pl.BlockSpec((None, bb, D_MODEL), lambda i, j: (l, i, k))

    def past_spec(half):
        return pl.BlockSpec((bb, None, 2, fc), lambda i, j: (i, l, 0, half * n_j + j))

    gpre_a, gpre_s = _layer_vec(g_pre, l)
    gpost_a, gpost_s = _layer_vec(g_post, l)
    b3 = b_conv.reshape(DEPTH, 1, 2 * D_FF)
    return pl.pallas_call(
        _ffn_sample_body,
        grid=(n_b // bb, n_j),
        in_specs=[
            pl.BlockSpec((n_t, bb, D_MODEL), lambda i, j: (0, i, 0)),
            mod_spec(3), mod_spec(4), mod_spec(5),
            gpre_s,
            *_ffn_weight_specs(l, fc),
            past_spec(0), past_spec(1),
            pl.BlockSpec((fc, D_MODEL), lambda i, j: (j, 0)),
            gpost_s,
        ],
        out_specs=[
            pl.BlockSpec((n_t, bb, D_MODEL), lambda i, j: (0, i, 0)),
            pl.BlockSpec((2, bb, fc), lambda i, j: (0, i, j)),
            pl.BlockSpec((2, bb, fc), lambda i, j: (0, i, j)),
        ],
        out_shape=[
            jax.ShapeDtypeStruct((n_t, n_b, D_MODEL), F32),
            jax.ShapeDtypeStruct((2, n_b, D_FF), F32),
            jax.ShapeDtypeStruct((2, n_b, D_FF), F32),
        ],
        scratch_shapes=[
            pltpu.VMEM((tm, D_MODEL), BF16),
            pltpu.VMEM((2 * bb + tm, fc), F32),
            pltpu.VMEM((2 * bb + tm, fc), F32),
            pltpu.VMEM((tm, fc), BF16),
            pltpu.VMEM((tm, D_MODEL), F32),
        ],
        compiler_params=pltpu.CompilerParams(
            dimension_semantics=("arbitrary", "arbitrary"), vmem_limit_bytes=VMEM_LIMIT_BYTES),
        name="ffn_sample",
    )(x, mod, mod, mod, gpre_a, w_up_bf, w_up_bf, w_conv, w_conv, b3, b3,
      state_ffn, state_ffn, w_down_bf, gpost_a)


def kernel(x_prompt, x_sample, state_conv_mix, state_conv_ffn, c_prompt, c_sample, w_ada, b_ada, g_pre_mix, g_post_mix, g_pre_ffn, g_post_ffn, w_in, g_v, w_spatial, b_spatial, w_conv_mix, g_out_a, g_out_b, w_out, w_up, w_conv_ffn, b_conv_ffn, w_down):
    n_seq, seq, _ = x_prompt.shape
    n_b, n_t, _ = x_sample.shape

    pad = (-(n_b + n_seq)) % 8
    c_all = jnp.concatenate([c_sample, c_prompt, jnp.zeros((pad, D_MODEL), F32)], axis=0)
    mod = _ada_call(c_all, w_ada, b_ada)
    mod_p = mod[:, n_b:n_b + n_seq].reshape(DEPTH, n_seq, 1, N_MOD * D_MODEL)

    w_up_bf, w_down_bf, w_in_bf, w_out_bf = [w[0].astype(BF16) for w in (w_up, w_down, w_in, w_out)]
    b_spatial_t = jnp.swapaxes(b_spatial, 1, 2)
    wsp_flat = w_spatial[:, :, :n_t, :n_t].reshape(-1)
    bsp_flat = b_spatial[:, :, :n_t].reshape(-1)
    state_mix = state_conv_mix.reshape(n_b, DEPTH * 2 * W_B)

    xp = x_prompt.reshape(n_seq * seq, D_MODEL)
    xs = jnp.swapaxes(x_sample, 0, 1)
    mix_p, mix_s, ffn_p, ffn_s, vr_p, vr_s = [], [], [], [], [], []
    for l in range(DEPTH):
        xp, nm, vr = _mixer_prompt_call(l, xp, mod_p, g_pre_mix, w_in_bf, g_v, w_spatial,
                                        b_spatial_t, w_conv_mix, g_out_a, g_out_b, w_out_bf,
                                        g_post_mix)
        mix_p.append(nm)
        vr_p.append(vr)
        xs, nm, vr = _mixer_sample_call(l, xs, mod, state_mix, g_pre_mix, w_in_bf, g_v, wsp_flat,
                                        bsp_flat, w_conv_mix, g_out_a, g_out_b, w_out_bf,
                                        g_post_mix)
        mix_s.append(jnp.swapaxes(nm, 0, 1))
        vr_s.append(jnp.swapaxes(vr, 0, 1))
        xs, nfg, nfv = _ffn_sample_call(l, xs, mod, state_conv_ffn, g_pre_ffn, w_up_bf, w_conv_ffn,
                                        b_conv_ffn, w_down_bf, g_post_ffn)
        ffn_s.append(jnp.swapaxes(jnp.concatenate([nfg, nfv], axis=-1), 0, 1))
        next_weights = (w_up, w_down, w_in, w_out) if l + 1 < DEPTH else None
        xp, nfg, nfv, casts = _ffn_prompt_call(l, xp, mod_p, g_pre_ffn, w_up_bf, w_conv_ffn,
                                               b_conv_ffn, w_down_bf, g_post_ffn, next_weights)
        tps = nfg.shape[0] // n_seq
        ffn_p.append(jnp.concatenate([nfg, nfv], axis=-1)[tps - 1::tps])
        if casts:
            w_up_bf, w_down_bf, w_in_bf, w_out_bf = casts

    return (xp.reshape(n_seq, seq, D_MODEL), jnp.swapaxes(xs, 0, 1),
            jnp.stack(mix_p, axis=1), jnp.stack(mix_s, axis=1),
            jnp.stack(ffn_p, axis=1), jnp.stack(ffn_s, axis=1),
            jnp.stack(vr_p, axis=1), jnp.stack(vr_s, axis=1))
```

```python
import functools

import jax
import jax.numpy as jnp
from jax import lax
from jax.experimental import pallas as pl
from jax.experimental.pallas import tpu as pltpu

F32 = jnp.float32
BF16 = jnp.bfloat16

D_MODEL = 2048
DEPTH = 4
CHUNK = 128
N_HEADS_A = 8
HEAD_DIM_A = 128
W_A = N_HEADS_A * HEAD_DIM_A
W_B = D_MODEL - W_A
D_FF = 5632
N_MOD = 6
IN_COLS = 2 * W_A + 3 * W_B
EPS = 1e-6

MIX_ROWS = 256
FFN_ROWS = 512
FFN_COLS = 512
NORM_ROWS = 32
CONV_ROWS = 64
FFN_ROW_BLOCK = 128
FFN_UP_AHEAD = 2
DOWN_COLS = 512
CAST_IN_BLOCKS = 10
CAST_OUT_BLOCKS = 8
ADA_COLS = 1024
VMEM_LIMIT_BYTES = 56 * 1024 * 1024


def _rms(x, g):
    ms = jnp.mean(x * x, axis=-1, keepdims=True)
    return x * lax.rsqrt(ms + EPS) * g


def _gelu(x):
    c = 0.7978845608028654
    return 0.5 * x * (1.0 + jnp.tanh(c * (x + 0.044715 * (x * x * x))))


def _silu(x):
    h = 0.5 * x
    return h + h * jnp.tanh(h)


def _dot(a, b):
    return jnp.dot(a, b, preferred_element_type=F32)


def _conv_rows(u, prev2, prev1, w_ref, b_ref):
    rows = lax.broadcasted_iota(jnp.int32, u.shape, 0)
    p1 = jnp.where(rows == 0, prev1, pltpu.roll(u, 1, 0))
    p2 = jnp.where(rows == 0, prev2, jnp.where(rows == 1, prev1, pltpu.roll(u, 2, 0)))
    y = w_ref[0:1, :] * p2 + w_ref[1:2, :] * p1 + w_ref[2:3, :] * u
    if b_ref is not None:
        y = y + b_ref[...]
    return y


def _conv_blocks(u, past0, past1, w_ref, b_ref, bb):
    t = u.shape[0]
    p1 = jnp.concatenate([past1, u[: t - bb]], axis=0)
    p2 = jnp.concatenate([past0, past1, u[: t - 2 * bb]], axis=0)
    y = w_ref[0:1, :] * p2 + w_ref[1:2, :] * p1 + w_ref[2:3, :] * u
    if b_ref is not None:
        y = y + b_ref[...]
    return y


def _ada_body(c_ref, w_ref, b_ref, o_ref):
    s = _silu(c_ref[...]).astype(BF16)
    o_ref[...] = _dot(s, w_ref[...].astype(BF16)) + b_ref[...]


def _ada_call(c_all, w_ada, b_ada):
    rows = c_all.shape[0]
    n_col = (N_MOD * D_MODEL) // ADA_COLS
    return pl.pallas_call(
        _ada_body,
        grid=(DEPTH, n_col),
        in_specs=[
            pl.BlockSpec((rows, D_MODEL), lambda l, n: (0, 0)),
            pl.BlockSpec((None, D_MODEL, ADA_COLS), lambda l, n: (l, 0, n)),
            pl.BlockSpec((None, 1, ADA_COLS), lambda l, n: (l, 0, n)),
        ],
        out_specs=pl.BlockSpec((None, rows, ADA_COLS), lambda l, n: (l, 0, n)),
        out_shape=jax.ShapeDtypeStruct((DEPTH, rows, N_MOD * D_MODEL), F32),
        compiler_params=pltpu.CompilerParams(
            dimension_semantics=("arbitrary", "arbitrary"),
            vmem_limit_bytes=VMEM_LIMIT_BYTES),
        name="ada_mod",
    )(c_all, w_ada, b_ada.reshape(DEPTH, 1, N_MOD * D_MODEL))


def _mixer_core(x, sh, sc, gt, gpre_ref, w_in_ref, g_v_ref, g_oa_ref, g_ob_ref, w_out_ref,
                gpost_ref, oa_s, spatial_fn, conv_fn):
    h = (_rms(x, gpre_ref[...]) * (1.0 + sc) + sh).astype(BF16)
    u = _gelu(_dot(h, w_in_ref[:, 0:W_A]))
    v = _rms(_gelu(_dot(h, w_in_ref[:, W_A:2 * W_A])), g_v_ref[...])
    spatial_fn(u, v)
    ra = _rms(oa_s[...], g_oa_ref[...]).astype(BF16)
    gate_b = _dot(h, w_in_ref[:, 2 * W_A:2 * W_A + W_B])
    gate_c = _dot(h, w_in_ref[:, 2 * W_A + W_B:2 * W_A + 2 * W_B])
    h_b = _dot(h, w_in_ref[:, 2 * W_A + 2 * W_B:])
    ci = gate_c * h_b
    rb = _rms(gate_b * conv_fn(ci), g_ob_ref[...]).astype(BF16)
    merged = _dot(ra, w_out_ref[0:W_A, :]) + _dot(rb, w_out_ref[W_A:, :])
    y = x + gt * _rms(merged, gpost_ref[...])
    return y, v, ci


def _mixer_prompt_body(tiles_per_seq, x_ref, sh_ref, sc_ref, gt_ref, gpre_ref, w_in_ref, g_v_ref,
                       wsp_ref, bsp_ref, wconv_ref, g_oa_ref, g_ob_ref, w_out_ref, gpost_ref,
                       y_ref, nmix_ref, vrows_ref, oa_s, carry_s):
    i = pl.program_id(0)
    tm = x_ref.shape[0]
    first = (i % tiles_per_seq) == 0

    def spatial_fn(u, v):
        vb = v.astype(BF16)
        tril = (lax.broadcasted_iota(jnp.int32, (CHUNK, CHUNK), 0)
                >= lax.broadcasted_iota(jnp.int32, (CHUNK, CHUNK), 1))
        for hd in range(N_HEADS_A):
            cols = slice(hd * HEAD_DIM_A, (hd + 1) * HEAD_DIM_A)
            wm = jnp.where(tril, wsp_ref[hd], 0.0).astype(BF16)
            bias = bsp_ref[:, hd:hd + 1]
            for c in range(tm // CHUNK):
                rows = slice(c * CHUNK, (c + 1) * CHUNK)
                mixed = _dot(wm, vb[rows, cols]) + bias
                oa_s[rows, cols] = u[rows, cols] * mixed

    def conv_fn(ci):
        prev2 = jnp.where(first, 0.0, carry_s[6:7, :])
        prev1 = jnp.where(first, 0.0, carry_s[7:8, :])
        return _conv_rows(ci, prev2, prev1, wconv_ref, None)

    y, v, ci = _mixer_core(x_ref[...], sh_ref[...], sc_ref[...], gt_ref[...], gpre_ref, w_in_ref,
                           g_v_ref, g_oa_ref, g_ob_ref, w_out_ref, gpost_ref, oa_s,
                           spatial_fn, conv_fn)
    y_ref[...] = y
    carry_s[...] = ci[tm - 8:, :]
    nmix_ref[...] = ci[tm - 2:, :]
    vrows_ref[...] = v[tm - CHUNK:, :]


def _mixer_sample_body(layer, x_ref, sh_ref, sc_ref, gt_ref, gpre_ref, w_in_ref, g_v_ref,
                       wsp_ref, bsp_ref, wconv_ref, past0_ref, past1_ref, g_oa_ref, g_ob_ref,
                       w_out_ref, gpost_ref, y_ref, nmix_ref, vrows_ref, oa_s):
    n_t, bb, _ = x_ref.shape
    tm = n_t * bb

    def rep(m_ref):
        return jnp.concatenate([m_ref[...]] * n_t, axis=0)

    def spatial_fn(u, v):
        for hd in range(N_HEADS_A):
            cols = slice(hd * HEAD_DIM_A, (hd + 1) * HEAD_DIM_A)
            base = (layer * N_HEADS_A + hd) * n_t
            for t in range(n_t):
                mixed = None
                for s in range(t + 1):
                    term = wsp_ref[(base + t) * n_t + s] * v[s * bb:(s + 1) * bb, cols]
                    mixed = term if mixed is None else mixed + term
                mixed = mixed + bsp_ref[base + t]
                oa_s[t * bb:(t + 1) * bb, cols] = u[t * bb:(t + 1) * bb, cols] * mixed

    def conv_fn(ci):
        return _conv_blocks(ci, past0_ref[...], past1_ref[...], wconv_ref, None, bb)

    x = x_ref[...].reshape(tm, D_MODEL)
    y, v, ci = _mixer_core(x, rep(sh_ref), rep(sc_ref), rep(gt_ref), gpre_ref, w_in_ref,
                           g_v_ref, g_oa_ref, g_ob_ref, w_out_ref, gpost_ref, oa_s,
                           spatial_fn, conv_fn)
    y_ref[...] = y.reshape(n_t, bb, D_MODEL)
    nmix_ref[...] = ci[tm - 2 * bb:, :].reshape(2, bb, W_B)
    vrows_ref[...] = v.reshape(n_t, bb, W_A)


def _layer_vec(arr, l):
    c = arr.shape[-1]
    return arr.reshape(DEPTH, 1, c), pl.BlockSpec((None, 1, c), lambda *_: (l, 0, 0))


def _resident(shape, index):
    return pl.BlockSpec(shape, lambda *_: index, pipeline_mode=pl.Buffered(1))


def _mixer_prompt_call(l, x, mod_p, g_pre, w_in_bf, g_v, w_spatial, b_spatial_t, w_conv, g_oa, g_ob,
                       w_out_bf, g_post):
    n_tok = x.shape[0]
    n_seq = mod_p.shape[1]
    seq = n_tok // n_seq
    tm = MIX_ROWS
    tps = seq // tm

    def mod_spec(k):
        return pl.BlockSpec((None, None, 1, D_MODEL), lambda i: (l, i // tps, 0, k))

    gpre_a, gpre_s = _layer_vec(g_pre, l)
    gv_a, gv_s = _layer_vec(g_v, l)
    goa_a, goa_s = _layer_vec(g_oa, l)
    gob_a, gob_s = _layer_vec(g_ob, l)
    gpost_a, gpost_s = _layer_vec(g_post, l)
    return pl.pallas_call(
        functools.partial(_mixer_prompt_body, tps),
        grid=(n_tok // tm,),
        in_specs=[
            pl.BlockSpec((tm, D_MODEL), lambda i: (i, 0)),
            mod_spec(0), mod_spec(1), mod_spec(2),
            gpre_s,
            _resident((D_MODEL, IN_COLS), (0, 0)),
            gv_s,
            pl.BlockSpec((None, N_HEADS_A, CHUNK, CHUNK), lambda i: (l, 0, 0, 0)),
            pl.BlockSpec((None, CHUNK, N_HEADS_A), lambda i: (l, 0, 0)),
            pl.BlockSpec((None, 3, W_B), lambda i: (l, 0, 0)),
            goa_s, gob_s,
            _resident((D_MODEL, D_MODEL), (0, 0)),
            gpost_s,
        ],
        out_specs=[
            pl.BlockSpec((tm, D_MODEL), lambda i: (i, 0)),
            pl.BlockSpec((None, 2, W_B), lambda i: (i // tps, 0, 0)),
            pl.BlockSpec((None, CHUNK, W_A), lambda i: (i // tps, 0, 0)),
        ],
        out_shape=[
            jax.ShapeDtypeStruct((n_tok, D_MODEL), F32),
            jax.ShapeDtypeStruct((n_seq, 2, W_B), F32),
            jax.ShapeDtypeStruct((n_seq, CHUNK, W_A), F32),
        ],
        scratch_shapes=[pltpu.VMEM((tm, W_A), F32), pltpu.VMEM((8, W_B), F32)],
        compiler_params=pltpu.CompilerParams(
            dimension_semantics=("arbitrary",), vmem_limit_bytes=VMEM_LIMIT_BYTES),
        name="mixer_prompt",
    )(x, mod_p, mod_p, mod_p, gpre_a, w_in_bf, gv_a, w_spatial, b_spatial_t, w_conv, goa_a, gob_a,
      w_out_bf, gpost_a)


def _mixer_sample_call(l, x, mod, state_mix, g_pre, w_in_bf, g_v, wsp_flat, bsp_flat, w_conv, g_oa,
                       g_ob, w_out_bf, g_post):
    n_t, n_b, _ = x.shape
    bb = MIX_ROWS // n_t
    tm = n_t * bb

    def mod_spec(k):
        return pl.BlockSpec((None, bb, D_MODEL), lambda i: (l, i, k))

    def past_spec(r):
        return pl.BlockSpec((bb, W_B), lambda i: (i, l * 2 + r))

    gpre_a, gpre_s = _layer_vec(g_pre, l)
    gv_a, gv_s = _layer_vec(g_v, l)
    goa_a, goa_s = _layer_vec(g_oa, l)
    gob_a, gob_s = _layer_vec(g_ob, l)
    gpost_a, gpost_s = _layer_vec(g_post, l)
    smem = pl.BlockSpec(memory_space=pltpu.SMEM)
    return pl.pallas_call(
        functools.partial(_mixer_sample_body, l),
        grid=(n_b // bb,),
        in_specs=[
            pl.BlockSpec((n_t, bb, D_MODEL), lambda i: (0, i, 0)),
            mod_spec(0), mod_spec(1), mod_spec(2),
            gpre_s,
            _resident((D_MODEL, IN_COLS), (0, 0)),
            gv_s,
            smem, smem,
            pl.BlockSpec((None, 3, W_B), lambda i: (l, 0, 0)),
            past_spec(0), past_spec(1),
            goa_s, gob_s,
            _resident((D_MODEL, D_MODEL), (0, 0)),
            gpost_s,
        ],
        out_specs=[
            pl.BlockSpec((n_t, bb, D_MODEL), lambda i: (0, i, 0)),
            pl.BlockSpec((2, bb, W_B), lambda i: (0, i, 0)),
            pl.BlockSpec((n_t, bb, W_A), lambda i: (0, i, 0)),
        ],
        out_shape=[
            jax.ShapeDtypeStruct((n_t, n_b, D_MODEL), F32),
            jax.ShapeDtypeStruct((2, n_b, W_B), F32),
            jax.ShapeDtypeStruct((n_t, n_b, W_A), F32),
        ],
        scratch_shapes=[pltpu.VMEM((tm, W_A), F32)],
        compiler_params=pltpu.CompilerParams(
            dimension_semantics=("arbitrary",), vmem_limit_bytes=VMEM_LIMIT_BYTES),
        name="mixer_sample",
    )(x, mod, mod, mod, gpre_a, w_in_bf, gv_a, wsp_flat, bsp_flat, w_conv, state_mix, state_mix,
      goa_a, gob_a, w_out_bf, gpost_a)


def _ffn_chunk_step(tm, halo, step, wg_ref, wv_ref, cwg_ref, cwv_ref, cbg_ref, cbv_ref, wd_ref,
                    h_s, upg_s, upv_s, act_s, acc_add):
    n_k = tm // FFN_ROW_BLOCK

    def up(k):
        rows = slice(k * FFN_ROW_BLOCK, (k + 1) * FFN_ROW_BLOCK)
        dst = slice(halo + k * FFN_ROW_BLOCK, halo + (k + 1) * FFN_ROW_BLOCK)
        h = h_s[rows, :]
        upg_s[dst, :] = _dot(h, wg_ref[...])
        upv_s[dst, :] = _dot(h, wv_ref[...])

    def conv(up_s, w_ref, b_ref, r, n):
        t2, t1, t0 = [up_s[pl.ds(halo - k * step + r, n), :] for k in (2, 1, 0)]
        return (w_ref[0:1, :] * t2 + w_ref[1:2, :] * t1 + w_ref[2:3, :] * t0) + b_ref[...]

    def gate_down(k):
        for r in range(k * FFN_ROW_BLOCK, (k + 1) * FFN_ROW_BLOCK, CONV_ROWS):
            cg = conv(upg_s, cwg_ref, cbg_ref, r, CONV_ROWS)
            cv = conv(upv_s, cwv_ref, cbv_ref, r, CONV_ROWS)
            act_s[r:r + CONV_ROWS, :] = (_silu(cg) * cv).astype(BF16)
        rows = slice(k * FFN_ROW_BLOCK, (k + 1) * FFN_ROW_BLOCK)
        for c in range(0, D_MODEL, DOWN_COLS):
            acc_add(k, c, _dot(act_s[rows, :], wd_ref[:, c:c + DOWN_COLS]))

    for k in range(min(FFN_UP_AHEAD, n_k)):
        up(k)
    for k in range(n_k):
        if k + FFN_UP_AHEAD < n_k:
            up(k + FFN_UP_AHEAD)
        gate_down(k)


def _ffn_prompt_body(tiles_per_seq, n_cast, x_ref, sh_ref, sc_ref, gt_ref, gpre_ref, wg_ref, wv_ref,
                     cwg_ref, cwv_ref, cbg_ref, cbv_ref, wd_ref, gpost_ref, *rest):
    cast_in, rest = rest[:n_cast], rest[n_cast:]
    o_ref, nfg_ref, nfv_ref = rest[:3]
    cast_out, rest = rest[3:3 + n_cast], rest[3 + n_cast:]
    h_s, upg_s, upv_s, act_s, carry_g, carry_v = rest
    i = pl.program_id(0)
    j = pl.program_id(1)
    n_j = pl.num_programs(1)
    tm = x_ref.shape[0]
    halo = 8
    first_tile = (i % tiles_per_seq) == 0

    for src, dst in zip(cast_in, cast_out):
        dst[...] = src[...].astype(BF16)

    @pl.when(j == 0)
    def _():
        for r in range(0, tm, NORM_ROWS):
            rows = slice(r, r + NORM_ROWS)
            h_s[rows, :] = (_rms(x_ref[rows, :], gpre_ref[...]) * (1.0 + sc_ref[...])
                            + sh_ref[...]).astype(BF16)
        o_ref[...] = jnp.zeros_like(o_ref)

    upg_s[0:halo, :] = jnp.where(first_tile, 0.0, carry_g[j])
    upv_s[0:halo, :] = jnp.where(first_tile, 0.0, carry_v[j])

    def acc_add(k, c, part):
        rows = slice(k * FFN_ROW_BLOCK, (k + 1) * FFN_ROW_BLOCK)
        o_ref[rows, c:c + part.shape[1]] += part

    _ffn_chunk_step(tm, halo, 1, wg_ref, wv_ref, cwg_ref, cwv_ref, cbg_ref, cbv_ref, wd_ref,
                    h_s, upg_s, upv_s, act_s, acc_add)
    carry_g[j] = upg_s[tm:tm + halo, :]
    carry_v[j] = upv_s[tm:tm + halo, :]
    nfg_ref[...] = upg_s[halo + tm - 2:halo + tm, :]
    nfv_ref[...] = upv_s[halo + tm - 2:halo + tm, :]

    @pl.when(j == n_j - 1)
    def _():
        for r in range(0, tm, NORM_ROWS):
            rows = slice(r, r + NORM_ROWS)
            o_ref[rows, :] = x_ref[rows, :] + gt_ref[...] * _rms(o_ref[rows, :], gpost_ref[...])


def _ffn_sample_body(x_ref, sh_ref, sc_ref, gt_ref, gpre_ref, wg_ref, wv_ref,
                     cwg_ref, cwv_ref, cbg_ref, cbv_ref, pg_ref, pv_ref,
                     wd_ref, gpost_ref, o_ref, nfg_ref, nfv_ref, h_s, upg_s, upv_s, act_s, acc_s):
    j = pl.program_id(1)
    n_j = pl.num_programs(1)
    n_t, bb, _ = x_ref.shape
    tm = n_t * bb
    halo = 2 * bb

    def passes():
        for t in range(n_t):
            for r in range(0, bb, NORM_ROWS):
                yield t, slice(r, r + NORM_ROWS), slice(t * bb + r, t * bb + r + NORM_ROWS)

    @pl.when(j == 0)
    def _():
        for t, rb, rows in passes():
            h_s[rows, :] = (_rms(x_ref[t, rb, :], gpre_ref[...]) * (1.0 + sc_ref[rb, :])
                            + sh_ref[rb, :]).astype(BF16)
        acc_s[...] = jnp.zeros_like(acc_s)

    for r in range(2):
        upg_s[r * bb:(r + 1) * bb, :] = pg_ref[:, r, :]
        upv_s[r * bb:(r + 1) * bb, :] = pv_ref[:, r, :]

    def acc_add(k, c, part):
        rows = slice(k * FFN_ROW_BLOCK, (k + 1) * FFN_ROW_BLOCK)
        acc_s[rows, c:c + part.shape[1]] += part

    _ffn_chunk_step(tm, halo, bb, wg_ref, wv_ref, cwg_ref, cwv_ref, cbg_ref, cbv_ref, wd_ref,
                    h_s, upg_s, upv_s, act_s, acc_add)
    for k in range(2):
        rows = slice(tm + k * bb, tm + (k + 1) * bb)
        nfg_ref[k] = upg_s[rows, :]
        nfv_ref[k] = upv_s[rows, :]

    @pl.when(j == n_j - 1)
    def _():
        for t, rb, rows in passes():
            o_ref[t, rb, :] = x_ref[t, rb, :] + gt_ref[rb, :] * _rms(acc_s[rows, :], gpost_ref[...])


def _ffn_weight_specs(l, fc):
    n_j = D_FF // fc
    return [
        pl.BlockSpec((D_MODEL, fc), lambda i, j: (0, j)),
        pl.BlockSpec((D_MODEL, fc), lambda i, j: (0, n_j + j)),
        pl.BlockSpec((None, 3, fc), lambda i, j: (l, 0, j)),
        pl.BlockSpec((None, 3, fc), lambda i, j: (l, 0, n_j + j)),
        pl.BlockSpec((None, 1, fc), lambda i, j: (l, 0, j)),
        pl.BlockSpec((None, 1, fc), lambda i, j: (l, 0, n_j + j)),
    ]


def _next_layer_cast_specs(l, n_tiles, n_j, weights):
    w_up, w_down, w_in, w_out = weights
    rows = D_MODEL // n_tiles
    in_blocks, out_blocks = CAST_IN_BLOCKS, CAST_OUT_BLOCKS
    assert in_blocks <= n_j and out_blocks <= n_j
    shapes = [
        (w_up, (rows, 2 * D_FF // n_j), lambda i, j: (i, j)),
        (w_down, (D_FF // n_j, rows), lambda i, j: (j, i)),
        (w_in, (rows, IN_COLS // in_blocks), lambda i, j: (i, jnp.minimum(j, in_blocks - 1))),
        (w_out, (rows, D_MODEL // out_blocks), lambda i, j: (i, jnp.minimum(j, out_blocks - 1))),
    ]
    in_specs, out_specs, out_shape = [], [], []
    for w, blk, idx in shapes:
        in_specs.append(pl.BlockSpec((None,) + blk, lambda i, j, idx=idx: (l + 1,) + idx(i, j)))
        out_specs.append(pl.BlockSpec(blk, idx))
        out_shape.append(jax.ShapeDtypeStruct(w.shape[1:], BF16))
    return in_specs, out_specs, out_shape


def _ffn_prompt_call(l, x, mod_p, g_pre, w_up_bf, w_conv, b_conv, w_down_bf, g_post, next_weights):
    n_tok = x.shape[0]
    n_seq = mod_p.shape[1]
    seq = n_tok // n_seq
    tm, fc = FFN_ROWS, FFN_COLS
    tps = seq // tm
    n_j = D_FF // fc
    halo = 8
    cast_in, cast_out, cast_shape = ([], [], [])
    if next_weights is not None:
        cast_in, cast_out, cast_shape = _next_layer_cast_specs(l, n_tok // tm, n_j, next_weights)

    def mod_spec(k):
        return pl.BlockSpec((None, None, 1, D_MODEL), lambda i, j: (l, i // tps, 0, k))

    gpre_a, gpre_s = _layer_vec(g_pre, l)
    gpost_a, gpost_s = _layer_vec(g_post, l)
    b3 = b_conv.reshape(DEPTH, 1, 2 * D_FF)
    outs = pl.pallas_call(
        functools.partial(_ffn_prompt_body, tps, len(cast_in)),
        grid=(n_tok // tm, n_j),
        in_specs=[
            pl.BlockSpec((tm, D_MODEL), lambda i, j: (i, 0)),
            mod_spec(3), mod_spec(4), mod_spec(5),
            gpre_s,
            *_ffn_weight_specs(l, fc),
            pl.BlockSpec((fc, D_MODEL), lambda i, j: (j, 0)),
            gpost_s,
            *cast_in,
        ],
        out_specs=[
            pl.BlockSpec((tm, D_MODEL), lambda i, j: (i, 0)),
            pl.BlockSpec((None, 2, fc), lambda i, j: (i, 0, j)),
            pl.BlockSpec((None, 2, fc), lambda i, j: (i, 0, j)),
            *cast_out,
        ],
        out_shape=[
            jax.ShapeDtypeStruct((n_tok, D_MODEL), F32),
            jax.ShapeDtypeStruct((n_tok // tm, 2, D_FF), F32),
            jax.ShapeDtypeStruct((n_tok // tm, 2, D_FF), F32),
            *cast_shape,
        ],
        scratch_shapes=[
            pltpu.VMEM((tm, D_MODEL), BF16),
            pltpu.VMEM((halo + tm, fc), F32),
            pltpu.VMEM((halo + tm, fc), F32),
            pltpu.VMEM((tm, fc), BF16),
            pltpu.VMEM((n_j, halo, fc), F32),
            pltpu.VMEM((n_j, halo, fc), F32),
        ],
        compiler_params=pltpu.CompilerParams(
            dimension_semantics=("arbitrary", "arbitrary"), vmem_limit_bytes=VMEM_LIMIT_BYTES),
        name="ffn_prompt",
    )(x, mod_p, mod_p, mod_p, gpre_a, w_up_bf, w_up_bf, w_conv, w_conv, b3, b3, w_down_bf, gpost_a,
      *(next_weights or ()))
    return outs[0], outs[1], outs[2], tuple(outs[3:])


def _ffn_sample_call(l, x, mod, state_ffn, g_pre, w_up_bf, w_conv, b_conv, w_down_bf, g_post):
    n_t, n_b, _ = x.shape
    fc = FFN_COLS
    bb = FFN_ROWS // n_t
    tm = n_t * bb
    n_j = D_FF // fc

    def mod_spec(k):
        return pl.BlockSpec((None, bb, D_MODEL), lambda i, j: (l, i, k))

    def past_spec(half):
        return pl.BlockSpec((bb, None, 2, fc), lambda i, j: (i, l, 0, half * n_j + j))

    gpre_a, gpre_s = _layer_vec(g_pre, l)
    gpost_a, gpost_s = _layer_vec(g_post, l)
    b3 = b_conv.reshape(DEPTH, 1, 2 * D_FF)
    return pl.pallas_call(
        _ffn_sample_body,
        grid=(n_b // bb, n_j),
        in_specs=[
            pl.BlockSpec((n_t, bb, D_MODEL), lambda i, j: (0, i, 0)),
            mod_spec(3), mod_spec(4), mod_spec(5),
            gpre_s,
            *_ffn_weight_specs(l, fc),
            past_spec(0), past_spec(1),
            pl.BlockSpec((fc, D_MODEL), lambda i, j: (j, 0)),
            gpost_s,
        ],
        out_specs=[
            pl.BlockSpec((n_t, bb, D_MODEL), lambda i, j: (0, i, 0)),
            pl.BlockSpec((2, bb, fc), lambda i, j: (0, i, j)),
            pl.BlockSpec((2, bb, fc), lambda i, j: (0, i, j)),
        ],
        out_shape=[
            jax.ShapeDtypeStruct((n_t, n_b, D_MODEL), F32),
            jax.ShapeDtypeStruct((2, n_b, D_FF), F32),
            jax.ShapeDtypeStruct((2, n_b, D_FF), F32),
        ],
        scratch_shapes=[
            pltpu.VMEM((tm, D_MODEL), BF16),
            pltpu.VMEM((2 * bb + tm, fc), F32),
            pltpu.VMEM((2 * bb + tm, fc), F32),
            pltpu.VMEM((tm, fc), BF16),
            pltpu.VMEM((tm, D_MODEL), F32),
        ],
        compiler_params=pltpu.CompilerParams(
            dimension_semantics=("arbitrary", "arbitrary"), vmem_limit_bytes=VMEM_LIMIT_BYTES),
        name="ffn_sample",
    )(x, mod, mod, mod, gpre_a, w_up_bf, w_up_bf, w_conv, w_conv, b3, b3,
      state_ffn, state_ffn, w_down_bf, gpost_a)


def kernel(x_prompt, x_sample, state_conv_mix, state_conv_ffn, c_prompt, c_sample, w_ada, b_ada, g_pre_mix, g_post_mix, g_pre_ffn, g_post_ffn, w_in, g_v, w_spatial, b_spatial, w_conv_mix, g_out_a, g_out_b, w_out, w_up, w_conv_ffn, b_conv_ffn, w_down):
    n_seq, seq, _ = x_prompt.shape
    n_b, n_t, _ = x_sample.shape

    pad = (-(n_b + n_seq)) % 8
    c_all = jnp.concatenate([c_sample, c_prompt, jnp.zeros((pad, D_MODEL), F32)], axis=0)
    mod = _ada_call(c_all, w_ada, b_ada)
    mod_p = mod[:, n_b:n_b + n_seq].reshape(DEPTH, n_seq, 1, N_MOD * D_MODEL)

    w_up_bf, w_down_bf, w_in_bf, w_out_bf = [w[0].astype(BF16) for w in (w_up, w_down, w_in, w_out)]
    b_spatial_t = jnp.swapaxes(b_spatial, 1, 2)
    wsp_flat = w_spatial[:, :, :n_t, :n_t].reshape(-1)
    bsp_flat = b_spatial[:, :, :n_t].reshape(-1)
    state_mix = state_conv_mix.reshape(n_b, DEPTH * 2 * W_B)

    xp = x_prompt.reshape(n_seq * seq, D_MODEL)
    xs = jnp.swapaxes(x_sample, 0, 1)
    mix_p, mix_s, ffn_p, ffn_s, vr_p, vr_s = [], [], [], [], [], []
    for l in range(DEPTH):
        xp, nm, vr = _mixer_prompt_call(l, xp, mod_p, g_pre_mix, w_in_bf, g_v, w_spatial,
                                        b_spatial_t, w_conv_mix, g_out_a, g_out_b, w_out_bf,
                                        g_post_mix)
        mix_p.append(nm)
        vr_p.append(vr)
        xs, nm, vr = _mixer_sample_call(l, xs, mod, state_mix, g_pre_mix, w_in_bf, g_v, wsp_flat,
                                        bsp_flat, w_conv_mix, g_out_a, g_out_b, w_out_bf,
                                        g_post_mix)
        mix_s.append(jnp.swapaxes(nm, 0, 1))
        vr_s.append(jnp.swapaxes(vr, 0, 1))
        xs, nfg, nfv = _ffn_sample_call(l, xs, mod, state_conv_ffn, g_pre_ffn, w_up_bf, w_conv_ffn,
                                        b_conv_ffn, w_down_bf, g_post_ffn)
        ffn_s.append(jnp.swapaxes(jnp.concatenate([nfg, nfv], axis=-1), 0, 1))
        next_weights = (w_up, w_down, w_in, w_out) if l + 1 < DEPTH else None
        xp, nfg, nfv, casts = _ffn_prompt_call(l, xp, mod_p, g_pre_ffn, w_up_bf, w_conv_ffn,
                                               b_conv_ffn, w_down_bf, g_post_ffn, next_weights)
        tps = nfg.shape[0] // n_seq
        ffn_p.append(jnp.concatenate([nfg, nfv], axis=-1)[tps - 1::tps])
        if casts:
            w_up_bf, w_down_bf, w_in_bf, w_out_bf = casts

    return (xp.reshape(n_seq, seq, D_MODEL), jnp.swapaxes(xs, 0, 1),
            jnp.stack(mix_p, axis=1), jnp.stack(mix_s, axis=1),
            jnp.stack(ffn_p, axis=1), jnp.stack(ffn_s, axis=1),
            jnp.stack(vr_p, axis=1), jnp.stack(vr_s, axis=1))
```

```python
import functools

import jax
import jax.numpy as jnp
from jax import lax
from jax.experimental import pallas as pl
from jax.experimental.pallas import tpu as pltpu

F32 = jnp.float32
BF16 = jnp.bfloat16

D_MODEL = 2048
DEPTH = 4
CHUNK = 128
N_HEADS_A = 8
HEAD_DIM_A = 128
W_A = N_HEADS_A * HEAD_DIM_A
W_B = D_MODEL - W_A
D_FF = 5632
N_MOD = 6
IN_COLS = 2 * W_A + 3 * W_B
EPS = 1e-6

MIX_ROWS = 256
FFN_ROWS = 512
FFN_COLS = 512
NORM_ROWS = 16
CONV_ROWS = 64
FFN_ROW_BLOCK = 256
FFN_UP_AHEAD = 1
DOWN_COLS = 2048
CAST_IN_BLOCKS = 10
CAST_OUT_BLOCKS = 8
ADA_COLS = 1024
VMEM_LIMIT_BYTES = 56 * 1024 * 1024


def _rms(x, g):
    ms = jnp.mean(x * x, axis=-1, keepdims=True)
    return x * lax.rsqrt(ms + EPS) * g


def _gelu(x):
    c = 0.7978845608028654
    return 0.5 * x * (1.0 + jnp.tanh(c * (x + 0.044715 * (x * x * x))))


def _silu(x):
    h = 0.5 * x
    return h + h * jnp.tanh(h)


def _dot(a, b):
    return jnp.dot(a, b, preferred_element_type=F32)


def _conv_rows(u, prev2, prev1, w_ref, b_ref):
    rows = lax.broadcasted_iota(jnp.int32, u.shape, 0)
    p1 = jnp.where(rows == 0, prev1, pltpu.roll(u, 1, 0))
    p2 = jnp.where(rows == 0, prev2, jnp.where(rows == 1, prev1, pltpu.roll(u, 2, 0)))
    y = w_ref[0:1, :] * p2 + w_ref[1:2, :] * p1 + w_ref[2:3, :] * u
    if b_ref is not None:
        y = y + b_ref[...]
    return y


def _conv_blocks(u, past0, past1, w_ref, b_ref, bb):
    t = u.shape[0]
    p1 = jnp.concatenate([past1, u[: t - bb]], axis=0)
    p2 = jnp.concatenate([past0, past1, u[: t - 2 * bb]], axis=0)
    y = w_ref[0:1, :] * p2 + w_ref[1:2, :] * p1 + w_ref[2:3, :] * u
    if b_ref is not None:
        y = y + b_ref[...]
    return y


def _ada_body(c_ref, w_ref, b_ref, o_ref):
    s = _silu(c_ref[...]).astype(BF16)
    o_ref[...] = _dot(s, w_ref[...].astype(BF16)) + b_ref[...]


def _ada_call(c_all, w_ada, b_ada):
    rows = c_all.shape[0]
    n_col = (N_MOD * D_MODEL) // ADA_COLS
    return pl.pallas_call(
        _ada_body,
        grid=(DEPTH, n_col),
        in_specs=[
            pl.BlockSpec((rows, D_MODEL), lambda l, n: (0, 0)),
            pl.BlockSpec((None, D_MODEL, ADA_COLS), lambda l, n: (l, 0, n)),
            pl.BlockSpec((None, 1, ADA_COLS), lambda l, n: (l, 0, n)),
        ],
        out_specs=pl.BlockSpec((None, rows, ADA_COLS), lambda l, n: (l, 0, n)),
        out_shape=jax.ShapeDtypeStruct((DEPTH, rows, N_MOD * D_MODEL), F32),
        compiler_params=pltpu.CompilerParams(
            dimension_semantics=("arbitrary", "arbitrary"),
            vmem_limit_bytes=VMEM_LIMIT_BYTES),
        name="ada_mod",
    )(c_all, w_ada, b_ada.reshape(DEPTH, 1, N_MOD * D_MODEL))


def _mixer_core(x, sh, sc, gt, gpre_ref, w_in_ref, g_v_ref, g_oa_ref, g_ob_ref, w_out_ref,
                gpost_ref, oa_s, spatial_fn, conv_fn):
    h = (_rms(x, gpre_ref[...] * (1.0 + sc)) + sh).astype(BF16)
    u = _gelu(_dot(h, w_in_ref[:, 0:W_A]))
    v = _rms(_gelu(_dot(h, w_in_ref[:, W_A:2 * W_A])), g_v_ref[...])
    spatial_fn(u, v)
    ra = _rms(oa_s[...], g_oa_ref[...]).astype(BF16)
    gate_b = _dot(h, w_in_ref[:, 2 * W_A:2 * W_A + W_B])
    gate_c = _dot(h, w_in_ref[:, 2 * W_A + W_B:2 * W_A + 2 * W_B])
    h_b = _dot(h, w_in_ref[:, 2 * W_A + 2 * W_B:])
    ci = gate_c * h_b
    rb = _rms(gate_b * conv_fn(ci), g_ob_ref[...]).astype(BF16)
    merged = _dot(ra, w_out_ref[0:W_A, :]) + _dot(rb, w_out_ref[W_A:, :])
    y = x + _rms(merged, gt * gpost_ref[...])
    return y, v, ci


def _mixer_prompt_body(tiles_per_seq, x_ref, sh_ref, sc_ref, gt_ref, gpre_ref, w_in_ref, g_v_ref,
                       wsp_ref, bsp_ref, wconv_ref, g_oa_ref, g_ob_ref, w_out_ref, gpost_ref,
                       y_ref, nmix_ref, vrows_ref, oa_s, carry_s):
    i = pl.program_id(0)
    tm = x_ref.shape[0]
    first = (i % tiles_per_seq) == 0

    def spatial_fn(u, v):
        vb = v.astype(BF16)
        tril = (lax.broadcasted_iota(jnp.int32, (CHUNK, CHUNK), 0)
                >= lax.broadcasted_iota(jnp.int32, (CHUNK, CHUNK), 1))
        for hd in range(N_HEADS_A):
            cols = slice(hd * HEAD_DIM_A, (hd + 1) * HEAD_DIM_A)
            wm = jnp.where(tril, wsp_ref[hd], 0.0).astype(BF16)
            bias = bsp_ref[:, hd:hd + 1]
            for c in range(tm // CHUNK):
                rows = slice(c * CHUNK, (c + 1) * CHUNK)
                mixed = _dot(wm, vb[rows, cols]) + bias
                oa_s[rows, cols] = u[rows, cols] * mixed

    def conv_fn(ci):
        prev2 = jnp.where(first, 0.0, carry_s[6:7, :])
        prev1 = jnp.where(first, 0.0, carry_s[7:8, :])
        return _conv_rows(ci, prev2, prev1, wconv_ref, None)

    y, v, ci = _mixer_core(x_ref[...], sh_ref[...], sc_ref[...], gt_ref[...], gpre_ref, w_in_ref,
                           g_v_ref, g_oa_ref, g_ob_ref, w_out_ref, gpost_ref, oa_s,
                           spatial_fn, conv_fn)
    y_ref[...] = y
    carry_s[...] = ci[tm - 8:, :]
    nmix_ref[...] = ci[tm - 2:, :]
    vrows_ref[...] = v[tm - CHUNK:, :]


def _mixer_sample_body(layer, x_ref, sh_ref, sc_ref, gt_ref, gpre_ref, w_in_ref, g_v_ref,
                       wsp_ref, bsp_ref, wconv_ref, past0_ref, past1_ref, g_oa_ref, g_ob_ref,
                       w_out_ref, gpost_ref, y_ref, nmix_ref, vrows_ref, oa_s):
    n_t, bb, _ = x_ref.shape
    tm = n_t * bb

    def rep(m_ref):
        return jnp.concatenate([m_ref[...]] * n_t, axis=0)

    def spatial_fn(u, v):
        for hd in range(N_HEADS_A):
            cols = slice(hd * HEAD_DIM_A, (hd + 1) * HEAD_DIM_A)
            base = (layer * N_HEADS_A + hd) * n_t
            for t in range(n_t):
                mixed = None
                for s in range(t + 1):
                    term = wsp_ref[(base + t) * n_t + s] * v[s * bb:(s + 1) * bb, cols]
                    mixed = term if mixed is None else mixed + term
                mixed = mixed + bsp_ref[base + t]
                oa_s[t * bb:(t + 1) * bb, cols] = u[t * bb:(t + 1) * bb, cols] * mixed

    def conv_fn(ci):
        return _conv_blocks(ci, past0_ref[...], past1_ref[...], wconv_ref, None, bb)

    x = x_ref[...].reshape(tm, D_MODEL)
    y, v, ci = _mixer_core(x, rep(sh_ref), rep(sc_ref), rep(gt_ref), gpre_ref, w_in_ref,
                           g_v_ref, g_oa_ref, g_ob_ref, w_out_ref, gpost_ref, oa_s,
                           spatial_fn, conv_fn)
    y_ref[...] = y.reshape(n_t, bb, D_MODEL)
    nmix_ref[...] = ci[tm - 2 * bb:, :].reshape(2, bb, W_B)
    vrows_ref[...] = v.reshape(n_t, bb, W_A)


def _layer_vec(arr, l):
    c = arr.shape[-1]
    return arr.reshape(DEPTH, 1, c), pl.BlockSpec((None, 1, c), lambda *_: (l, 0, 0))


def _resident(shape, index):
    return pl.BlockSpec(shape, lambda *_: index, pipeline_mode=pl.Buffered(1))


def _mixer_prompt_call(l, x, mod_p, g_pre, w_in_bf, g_v, w_spatial, b_spatial_t, w_conv, g_oa, g_ob,
                       w_out_bf, g_post):
    n_tok = x.shape[0]
    n_seq = mod_p.shape[1]
    seq = n_tok // n_seq
    tm = MIX_ROWS
    tps = seq // tm

    def mod_spec(k):
        return pl.BlockSpec((None, None, 1, D_MODEL), lambda i: (l, i // tps, 0, k))

    gpre_a, gpre_s = _layer_vec(g_pre, l)
    gv_a, gv_s = _layer_vec(g_v, l)
    goa_a, goa_s = _layer_vec(g_oa, l)
    gob_a, gob_s = _layer_vec(g_ob, l)
    gpost_a, gpost_s = _layer_vec(g_post, l)
    return pl.pallas_call(
        functools.partial(_mixer_prompt_body, tps),
        grid=(n_tok // tm,),
        in_specs=[
            pl.BlockSpec((tm, D_MODEL), lambda i: (i, 0)),
            mod_spec(0), mod_spec(1), mod_spec(2),
            gpre_s,
            _resident((D_MODEL, IN_COLS), (0, 0)),
            gv_s,
            pl.BlockSpec((None, N_HEADS_A, CHUNK, CHUNK), lambda i: (l, 0, 0, 0)),
            pl.BlockSpec((None, CHUNK, N_HEADS_A), lambda i: (l, 0, 0)),
            pl.BlockSpec((None, 3, W_B), lambda i: (l, 0, 0)),
            goa_s, gob_s,
            _resident((D_MODEL, D_MODEL), (0, 0)),
            gpost_s,
        ],
        out_specs=[
            pl.BlockSpec((tm, D_MODEL), lambda i: (i, 0)),
            pl.BlockSpec((None, 2, W_B), lambda i: (i // tps, 0, 0)),
            pl.BlockSpec((None, CHUNK, W_A), lambda i: (i // tps, 0, 0)),
        ],
        out_shape=[
            jax.ShapeDtypeStruct((n_tok, D_MODEL), F32),
            jax.ShapeDtypeStruct((n_seq, 2, W_B), F32),
            jax.ShapeDtypeStruct((n_seq, CHUNK, W_A), F32),
        ],
        scratch_shapes=[pltpu.VMEM((tm, W_A), F32), pltpu.VMEM((8, W_B), F32)],
        compiler_params=pltpu.CompilerParams(
            dimension_semantics=("arbitrary",), vmem_limit_bytes=VMEM_LIMIT_BYTES),
        name="mixer_prompt",
    )(x, mod_p, mod_p, mod_p, gpre_a, w_in_bf, gv_a, w_spatial, b_spatial_t, w_conv, goa_a, gob_a,
      w_out_bf, gpost_a)


def _mixer_sample_call(l, x, mod, state_mix, g_pre, w_in_bf, g_v, wsp_flat, bsp_flat, w_conv, g_oa,
                       g_ob, w_out_bf, g_post):
    n_t, n_b, _ = x.shape
    bb = MIX_ROWS // n_t
    tm = n_t * bb

    def mod_spec(k):
        return pl.BlockSpec((None, bb, D_MODEL), lambda i: (l, i, k))

    def past_spec(r):
        return pl.BlockSpec((bb, W_B), lambda i: (i, l * 2 + r))

    gpre_a, gpre_s = _layer_vec(g_pre, l)
    gv_a, gv_s = _layer_vec(g_v, l)
    goa_a, goa_s = _layer_vec(g_oa, l)
    gob_a, gob_s = _layer_vec(g_ob, l)
    gpost_a, gpost_s = _layer_vec(g_post, l)
    smem = pl.BlockSpec(memory_space=pltpu.SMEM)
    return pl.pallas_call(
        functools.partial(_mixer_sample_body, l),
        grid=(n_b // bb,),
        in_specs=[
            pl.BlockSpec((n_t, bb, D_MODEL), lambda i: (0, i, 0)),
            mod_spec(0), mod_spec(1), mod_spec(2),
            gpre_s,
            _resident((D_MODEL, IN_COLS), (0, 0)),
            gv_s,
            smem, smem,
            pl.BlockSpec((None, 3, W_B), lambda i: (l, 0, 0)),
            past_spec(0), past_spec(1),
            goa_s, gob_s,
            _resident((D_MODEL, D_MODEL), (0, 0)),
            gpost_s,
        ],
        out_specs=[
            pl.BlockSpec((n_t, bb, D_MODEL), lambda i: (0, i, 0)),
            pl.BlockSpec((2, bb, W_B), lambda i: (0, i, 0)),
            pl.BlockSpec((n_t, bb, W_A), lambda i: (0, i, 0)),
        ],
        out_shape=[
            jax.ShapeDtypeStruct((n_t, n_b, D_MODEL), F32),
            jax.ShapeDtypeStruct((2, n_b, W_B), F32),
            jax.ShapeDtypeStruct((n_t, n_b, W_A), F32),
        ],
        scratch_shapes=[pltpu.VMEM((tm, W_A), F32)],
        compiler_params=pltpu.CompilerParams(
            dimension_semantics=("arbitrary",), vmem_limit_bytes=VMEM_LIMIT_BYTES),
        name="mixer_sample",
    )(x, mod, mod, mod, gpre_a, w_in_bf, gv_a, wsp_flat, bsp_flat, w_conv, state_mix, state_mix,
      goa_a, gob_a, w_out_bf, gpost_a)


def _ffn_chunk_step(tm, halo, step, wg_ref, wv_ref, cwg_ref, cwv_ref, cbg_ref, cbv_ref, wd_ref,
                    h_s, upg_s, upv_s, act_s, acc_add):
    n_k = tm // FFN_ROW_BLOCK

    def up(k):
        rows = slice(k * FFN_ROW_BLOCK, (k + 1) * FFN_ROW_BLOCK)
        dst = slice(halo + k * FFN_ROW_BLOCK, halo + (k + 1) * FFN_ROW_BLOCK)
        h = h_s[rows, :]
        upg_s[dst, :] = _dot(h, wg_ref[...])
        upv_s[dst, :] = _dot(h, wv_ref[...])

    def conv(up_s, w_ref, b_ref, r, n):
        t2, t1, t0 = [up_s[pl.ds(halo - k * step + r, n), :] for k in (2, 1, 0)]
        return (w_ref[0:1, :] * t2 + w_ref[1:2, :] * t1 + w_ref[2:3, :] * t0) + b_ref[...]

    def gate_down(k):
        for r in range(k * FFN_ROW_BLOCK, (k + 1) * FFN_ROW_BLOCK, CONV_ROWS):
            cg = conv(upg_s, cwg_ref, cbg_ref, r, CONV_ROWS)
            cv = conv(upv_s, cwv_ref, cbv_ref, r, CONV_ROWS)
            act_s[r:r + CONV_ROWS, :] = (_silu(cg) * cv).astype(BF16)
        rows = slice(k * FFN_ROW_BLOCK, (k + 1) * FFN_ROW_BLOCK)
        for c in range(0, D_MODEL, DOWN_COLS):
            acc_add(k, c, _dot(act_s[rows, :], wd_ref[:, c:c + DOWN_COLS]))

    for k in range(min(FFN_UP_AHEAD, n_k)):
        up(k)
    for k in range(n_k):
        if k + FFN_UP_AHEAD < n_k:
            up(k + FFN_UP_AHEAD)
        gate_down(k)


def _ffn_prompt_body(tiles_per_seq, n_cast, x_ref, sh_ref, sc_ref, gt_ref, gpre_ref, wg_ref, wv_ref,
                     cwg_ref, cwv_ref, cbg_ref, cbv_ref, wd_ref, gpost_ref, *rest):
    cast_in, rest = rest[:n_cast], rest[n_cast:]
    o_ref, nfg_ref, nfv_ref = rest[:3]
    cast_out, rest = rest[3:3 + n_cast], rest[3 + n_cast:]
    h_s, upg_s, upv_s, act_s, carry_g, carry_v = rest
    i = pl.program_id(0)
    j = pl.program_id(1)
    n_j = pl.num_programs(1)
    tm = x_ref.shape[0]
    halo = 8
    first_tile = (i % tiles_per_seq) == 0

    for src, dst in zip(cast_in, cast_out):
        dst[...] = src[...].astype(BF16)

    @pl.when(j == 0)
    def _():
        gain = gpre_ref[...] * (1.0 + sc_ref[...])
        for r in range(0, tm, NORM_ROWS):
            rows = slice(r, r + NORM_ROWS)
            h_s[rows, :] = (_rms(x_ref[rows, :], gain) + sh_ref[...]).astype(BF16)
        o_ref[...] = jnp.zeros_like(o_ref)

    upg_s[0:halo, :] = jnp.where(first_tile, 0.0, carry_g[j])
    upv_s[0:halo, :] = jnp.where(first_tile, 0.0, carry_v[j])

    def acc_add(k, c, part):
        rows = slice(k * FFN_ROW_BLOCK, (k + 1) * FFN_ROW_BLOCK)
        o_ref[rows, c:c + part.shape[1]] += part

    _ffn_chunk_step(tm, halo, 1, wg_ref, wv_ref, cwg_ref, cwv_ref, cbg_ref, cbv_ref, wd_ref,
                    h_s, upg_s, upv_s, act_s, acc_add)
    carry_g[j] = upg_s[tm:tm + halo, :]
    carry_v[j] = upv_s[tm:tm + halo, :]
    nfg_ref[...] = upg_s[halo + tm - 2:halo + tm, :]
    nfv_ref[...] = upv_s[halo + tm - 2:halo + tm, :]

    @pl.when(j == n_j - 1)
    def _():
        gain = gt_ref[...] * gpost_ref[...]
        for r in range(0, tm, NORM_ROWS):
            rows = slice(r, r + NORM_ROWS)
            o_ref[rows, :] = x_ref[rows, :] + _rms(o_ref[rows, :], gain)


def _ffn_sample_body(x_ref, sh_ref, sc_ref, gt_ref, gpre_ref, wg_ref, wv_ref,
                     cwg_ref, cwv_ref, cbg_ref, cbv_ref, pg_ref, pv_ref,
                     wd_ref, gpost_ref, o_ref, nfg_ref, nfv_ref, h_s, upg_s, upv_s, act_s, acc_s):
    j = pl.program_id(1)
    n_j = pl.num_programs(1)
    n_t, bb, _ = x_ref.shape
    tm = n_t * bb
    halo = 2 * bb

    def passes():
        for t in range(n_t):
            for r in range(0, bb, NORM_ROWS):
                yield t, slice(r, r + NORM_ROWS), slice(t * bb + r, t * bb + r + NORM_ROWS)

    @pl.when(j == 0)
    def _():
        for t, rb, rows in passes():
            h_s[rows, :] = (_rms(x_ref[t, rb, :], gpre_ref[...]) * (1.0 + sc_ref[rb, :])
                            + sh_ref[rb, :]).astype(BF16)
        acc_s[...] = jnp.zeros_like(acc_s)

    for r in range(2):
        upg_s[r * bb:(r + 1) * bb, :] = pg_ref[:, r, :]
        upv_s[r * bb:(r + 1) * bb, :] = pv_ref[:, r, :]

    def acc_add(k, c, part):
        rows = slice(k * FFN_ROW_BLOCK, (k + 1) * FFN_ROW_BLOCK)
        acc_s[rows, c:c + part.shape[1]] += part

    _ffn_chunk_step(tm, halo, bb, wg_ref, wv_ref, cwg_ref, cwv_ref, cbg_ref, cbv_ref, wd_ref,
                    h_s, upg_s, upv_s, act_s, acc_add)
    for k in range(2):
        rows = slice(tm + k * bb, tm + (k + 1) * bb)
        nfg_ref[k] = upg_s[rows, :]
        nfv_ref[k] = upv_s[rows, :]

    @pl.when(j == n_j - 1)
    def _():
        for t, rb, rows in passes():
            o_ref[t, rb, :] = x_ref[t, rb, :] + gt_ref[rb, :] * _rms(acc_s[rows, :], gpost_ref[...])


def _ffn_weight_specs(l, fc):
    n_j = D_FF // fc
    return [
        pl.BlockSpec((D_MODEL, fc), lambda i, j: (0, j)),
        pl.BlockSpec((D_MODEL, fc), lambda i, j: (0, n_j + j)),
        pl.BlockSpec((None, 3, fc), lambda i, j: (l, 0, j)),
        pl.BlockSpec((None, 3, fc), lambda i, j: (l, 0, n_j + j)),
        pl.BlockSpec((None, 1, fc), lambda i, j: (l, 0, j)),
        pl.BlockSpec((None, 1, fc), lambda i, j: (l, 0, n_j + j)),
    ]


def _next_layer_cast_specs(l, n_tiles, n_j, weights):
    w_up, w_down, w_in, w_out = weights
    rows = D_MODEL // n_tiles
    in_blocks, out_blocks = CAST_IN_BLOCKS, CAST_OUT_BLOCKS
    assert in_blocks <= n_j and out_blocks <= n_j
    shapes = [
        (w_up, (rows, 2 * D_FF // n_j), lambda i, j: (i, j)),
        (w_down, (D_FF // n_j, rows), lambda i, j: (j, i)),
        (w_in, (rows, IN_COLS // in_blocks), lambda i, j: (i, jnp.minimum(j, in_blocks - 1))),
        (w_out, (rows, D_MODEL // out_blocks), lambda i, j: (i, jnp.minimum(j, out_blocks - 1))),
    ]
    in_specs, out_specs, out_shape = [], [], []
    for w, blk, idx in shapes:
        in_specs.append(pl.BlockSpec((None,) + blk, lambda i, j, idx=idx: (l + 1,) + idx(i, j)))
        out_specs.append(pl.BlockSpec(blk, idx))
        out_shape.append(jax.ShapeDtypeStruct(w.shape[1:], BF16))
    return in_specs, out_specs, out_shape


def _ffn_prompt_call(l, x, mod_p, g_pre, w_up_bf, w_conv, b_conv, w_down_bf, g_post, next_weights):
    n_tok = x.shape[0]
    n_seq = mod_p.shape[1]
    seq = n_tok // n_seq
    tm, fc = FFN_ROWS, FFN_COLS
    tps = seq // tm
    n_j = D_FF // fc
    halo = 8
    cast_in, cast_out, cast_shape = ([], [], [])
    if next_weights is not None:
        cast_in, cast_out, cast_shape = _next_layer_cast_specs(l, n_tok // tm, n_j, next_weights)

    def mod_spec(k):
        return pl.BlockSpec((None, None, 1, D_MODEL), lambda i, j: (l, i // tps, 0, k))

    gpre_a, gpre_s = _layer_vec(g_pre, l)
    gpost_a, gpost_s = _layer_vec(g_post, l)
    b3 = b_conv.reshape(DEPTH, 1, 2 * D_FF)
    outs = pl.pallas_call(
        functools.partial(_ffn_prompt_body, tps, len(cast_in)),
        grid=(n_tok // tm, n_j),
        in_specs=[
            pl.BlockSpec((tm, D_MODEL), lambda i, j: (i, 0)),
            mod_spec(3), mod_spec(4), mod_spec(5),
            gpre_s,
            *_ffn_weight_specs(l, fc),
            pl.BlockSpec((fc, D_MODEL), lambda i, j: (j, 0)),
            gpost_s,
            *cast_in,
        ],
        out_specs=[
            pl.BlockSpec((tm, D_MODEL), lambda i, j: (i, 0)),
            pl.BlockSpec((None, 2, fc), lambda i, j: (i, 0, j)),
            pl.BlockSpec((None, 2, fc), lambda i, j: (i, 0, j)),
            *cast_out,
        ],
        out_shape=[
            jax.ShapeDtypeStruct((n_tok, D_MODEL), F32),
            jax.ShapeDtypeStruct((n_tok // tm, 2, D_FF), F32),
            jax.ShapeDtypeStruct((n_tok // tm, 2, D_FF), F32),
            *cast_shape,
        ],
        scratch_shapes=[
            pltpu.VMEM((tm, D_MODEL), BF16),
            pltpu.VMEM((halo + tm, fc), F32),
            pltpu.VMEM((halo + tm, fc), F32),
            pltpu.VMEM((tm, fc), BF16),
            pltpu.VMEM((n_j, halo, fc), F32),
            pltpu.VMEM((n_j, halo, fc), F32),
        ],
        compiler_params=pltpu.CompilerParams(
            dimension_semantics=("arbitrary", "arbitrary"), vmem_limit_bytes=VMEM_LIMIT_BYTES),
        name="ffn_prompt",
    )(x, mod_p, mod_p, mod_p, gpre_a, w_up_bf, w_up_bf, w_conv, w_conv, b3, b3, w_down_bf, gpost_a,
      *(next_weights or ()))
    return outs[0], outs[1], outs[2], tuple(outs[3:])


def _ffn_sample_call(l, x, mod, state_ffn, g_pre, w_up_bf, w_conv, b_conv, w_down_bf, g_post):
    n_t, n_b, _ = x.shape
    fc = FFN_COLS
    bb = FFN_ROWS // n_t
    tm = n_t * bb
    n_j = D_FF // fc

    def mod_spec(k):
        return pl.BlockSpec((None, bb, D_MODEL), lambda i, j: (l, i, k))

    def past_spec(half):
        return pl.BlockSpec((bb, None, 2, fc), lambda i, j: (i, l, 0, half * n_j + j))

    gpre_a, gpre_s = _layer_vec(g_pre, l)
    gpost_a, gpost_s = _layer_vec(g_post, l)
    b3 = b_conv.reshape(DEPTH, 1, 2 * D_FF)
    return pl.pallas_call(
        _ffn_sample_body,
        grid=(n_b // bb, n_j),
        in_specs=[
            pl.BlockSpec((n_t, bb, D_MODEL), lambda i, j: (0, i, 0)),
            mod_spec(3), mod_spec(4), mod_spec(5),
            gpre_s,
            *_ffn_weight_specs(l, fc),
            past_spec(0), past_spec(1),
            pl.BlockSpec((fc, D_MODEL), lambda i, j: (j, 0)),
            gpost_s,
        ],
        out_specs=[
            pl.BlockSpec((n_t, bb, D_MODEL), lambda i, j: (0, i, 0)),
            pl.BlockSpec((2, bb, fc), lambda i, j: (0, i, j)),
            pl.BlockSpec((2, bb, fc), lambda i, j: (0, i, j)),
        ],
        out_shape=[
            jax.ShapeDtypeStruct((n_t, n_b, D_MODEL), F32),
            jax.ShapeDtypeStruct((2, n_b, D_FF), F32),
            jax.ShapeDtypeStruct((2, n_b, D_FF), F32),
        ],
        scratch_shapes=[
            pltpu.VMEM((tm, D_MODEL), BF16),
            pltpu.VMEM((2 * bb + tm, fc), F32),
            pltpu.VMEM((2 * bb + tm, fc), F32),
            pltpu.VMEM((tm, fc), BF16),
            pltpu.VMEM((tm, D_MODEL), F32),
        ],
        compiler_params=pltpu.CompilerParams(
            dimension_semantics=("arbitrary", "arbitrary"), vmem_limit_bytes=VMEM_LIMIT_BYTES),
        name="ffn_sample",
    )(x, mod, mod, mod, gpre_a, w_up_bf, w_up_bf, w_conv, w_conv, b3, b3,
      state_ffn, state_ffn, w_down_bf, gpost_a)


def kernel(x_prompt, x_sample, state_conv_mix, state_conv_ffn, c_prompt, c_sample, w_ada, b_ada, g_pre_mix, g_post_mix, g_pre_ffn, g_post_ffn, w_in, g_v, w_spatial, b_spatial, w_conv_mix, g_out_a, g_out_b, w_out, w_up, w_conv_ffn, b_conv_ffn, w_down):
    n_seq, seq, _ = x_prompt.shape
    n_b, n_t, _ = x_sample.shape

    pad = (-(n_b + n_seq)) % 8
    c_all = jnp.concatenate([c_sample, c_prompt, jnp.zeros((pad, D_MODEL), F32)], axis=0)
    mod = _ada_call(c_all, w_ada, b_ada)
    mod_p = mod[:, n_b:n_b + n_seq].reshape(DEPTH, n_seq, 1, N_MOD * D_MODEL)

    w_up_bf, w_down_bf, w_in_bf, w_out_bf = [w[0].astype(BF16) for w in (w_up, w_down, w_in, w_out)]
    b_spatial_t = jnp.swapaxes(b_spatial, 1, 2)
    wsp_flat = w_spatial[:, :, :n_t, :n_t].reshape(-1)
    bsp_flat = b_spatial[:, :, :n_t].reshape(-1)
    state_mix = state_conv_mix.reshape(n_b, DEPTH * 2 * W_B)

    xp = x_prompt.reshape(n_seq * seq, D_MODEL)
    xs = jnp.swapaxes(x_sample, 0, 1)
    mix_p, mix_s, ffn_p, ffn_s, vr_p, vr_s = [], [], [], [], [], []
    for l in range(DEPTH):
        xp, nm, vr = _mixer_prompt_call(l, xp, mod_p, g_pre_mix, w_in_bf, g_v, w_spatial,
                                        b_spatial_t, w_conv_mix, g_out_a, g_out_b, w_out_bf,
                                        g_post_mix)
        mix_p.append(nm)
        vr_p.append(vr)
        xs, nm, vr = _mixer_sample_call(l, xs, mod, state_mix, g_pre_mix, w_in_bf, g_v, wsp_flat,
                                        bsp_flat, w_conv_mix, g_out_a, g_out_b, w_out_bf,
                                        g_post_mix)
        mix_s.append(jnp.swapaxes(nm, 0, 1))
        vr_s.append(jnp.swapaxes(vr, 0, 1))
        xs, nfg, nfv = _ffn_sample_call(l, xs, mod, state_conv_ffn, g_pre_ffn, w_up_bf, w_conv_ffn,
                                        b_conv_ffn, w_down_bf, g_post_ffn)
        ffn_s.append(jnp.swapaxes(jnp.concatenate([nfg, nfv], axis=-1), 0, 1))
        next_weights = (w_up, w_down, w_in, w_out) if l + 1 < DEPTH else None
        xp, nfg, nfv, casts = _ffn_prompt_call(l, xp, mod_p, g_pre_ffn, w_up_bf, w_conv_ffn,
                                               b_conv_ffn, w_down_bf, g_post_ffn, next_weights)
        tps = nfg.shape[0] // n_seq
        ffn_p.append(jnp.concatenate([nfg, nfv], axis=-1)[tps - 1::tps])
        if casts:
            w_up_bf, w_down_bf, w_in_bf, w_out_bf = casts

    return (xp.reshape(n_seq, seq, D_MODEL), jnp.swapaxes(xs, 0, 1),
            jnp.stack(mix_p, axis=1), jnp.stack(mix_s, axis=1),
            jnp.stack(ffn_p, axis=1), jnp.stack(ffn_s, axis=1),
            jnp.stack(vr_p, axis=1), jnp.stack(vr_s, axis=1))
```

```python
import functools

import jax
import jax.numpy as jnp
from jax import lax
from jax.experimental import pallas as pl
from jax.experimental.pallas import tpu as pltpu

F32 = jnp.float32
BF16 = jnp.bfloat16

D_MODEL = 2048
DEPTH = 4
CHUNK = 128
N_HEADS_A = 8
HEAD_DIM_A = 128
W_A = N_HEADS_A * HEAD_DIM_A
W_B = D_MODEL - W_A
D_FF = 5632
N_MOD = 6
IN_COLS = 2 * W_A + 3 * W_B
EPS = 1e-6

MIX_ROWS = 256
FFN_ROWS = 512
FFN_COLS = 512
NORM_ROWS = 16
CONV_ROWS = 64
FFN_ROW_BLOCK = 256
FFN_UP_AHEAD = 1
DOWN_COLS = 2048
CAST_IN_BLOCKS = 10
CAST_OUT_BLOCKS = 8
ADA_COLS = 1024
VMEM_LIMIT_BYTES = 56 * 1024 * 1024


def _rms(x, g):
    ms = jnp.mean(x * x, axis=-1, keepdims=True)
    return x * lax.rsqrt(ms + EPS) * g


def _gelu(x):
    c = 0.7978845608028654
    return 0.5 * x * (1.0 + jnp.tanh(c * (x + 0.044715 * (x * x * x))))


def _silu(x):
    h = 0.5 * x
    return h + h * jnp.tanh(h)


def _dot(a, b):
    return jnp.dot(a, b, preferred_element_type=F32)


def _conv_rows(u, prev2, prev1, w_ref, b_ref):
    rows = lax.broadcasted_iota(jnp.int32, u.shape, 0)
    p1 = jnp.where(rows == 0, prev1, pltpu.roll(u, 1, 0))
    p2 = jnp.where(rows == 0, prev2, jnp.where(rows == 1, prev1, pltpu.roll(u, 2, 0)))
    y = w_ref[0:1, :] * p2 + w_ref[1:2, :] * p1 + w_ref[2:3, :] * u
    if b_ref is not None:
        y = y + b_ref[...]
    return y


def _conv_blocks(u, past0, past1, w_ref, b_ref, bb):
    t = u.shape[0]
    p1 = jnp.concatenate([past1, u[: t - bb]], axis=0)
    p2 = jnp.concatenate([past0, past1, u[: t - 2 * bb]], axis=0)
    y = w_ref[0:1, :] * p2 + w_ref[1:2, :] * p1 + w_ref[2:3, :] * u
    if b_ref is not None:
        y = y + b_ref[...]
    return y


def _ada_body(c_ref, w_ref, b_ref, o_ref):
    s = _silu(c_ref[...]).astype(BF16)
    o_ref[...] = _dot(s, w_ref[...].astype(BF16)) + b_ref[...]


def _ada_call(c_all, w_ada, b_ada):
    rows = c_all.shape[0]
    n_col = (N_MOD * D_MODEL) // ADA_COLS
    return pl.pallas_call(
        _ada_body,
        grid=(DEPTH, n_col),
        in_specs=[
            pl.BlockSpec((rows, D_MODEL), lambda l, n: (0, 0)),
            pl.BlockSpec((None, D_MODEL, ADA_COLS), lambda l, n: (l, 0, n)),
            pl.BlockSpec((None, 1, ADA_COLS), lambda l, n: (l, 0, n)),
        ],
        out_specs=pl.BlockSpec((None, rows, ADA_COLS), lambda l, n: (l, 0, n)),
        out_shape=jax.ShapeDtypeStruct((DEPTH, rows, N_MOD * D_MODEL), F32),
        compiler_params=pltpu.CompilerParams(
            dimension_semantics=("arbitrary", "arbitrary"),
            vmem_limit_bytes=VMEM_LIMIT_BYTES),
        name="ada_mod",
    )(c_all, w_ada, b_ada.reshape(DEPTH, 1, N_MOD * D_MODEL))


def _mixer_core(x, sh, sc, gt, gpre_ref, w_in_ref, g_v_ref, g_oa_ref, g_ob_ref, w_out_ref,
                gpost_ref, oa_s, spatial_fn, conv_fn):
    h = (_rms(x, gpre_ref[...] * (1.0 + sc)) + sh).astype(BF16)
    u = _gelu(_dot(h, w_in_ref[:, 0:W_A]))
    v = _rms(_gelu(_dot(h, w_in_ref[:, W_A:2 * W_A])), g_v_ref[...])
    spatial_fn(u, v)
    ra = _rms(oa_s[...], g_oa_ref[...]).astype(BF16)
    gate_b = _dot(h, w_in_ref[:, 2 * W_A:2 * W_A + W_B])
    gate_c = _dot(h, w_in_ref[:, 2 * W_A + W_B:2 * W_A + 2 * W_B])
    h_b = _dot(h, w_in_ref[:, 2 * W_A + 2 * W_B:])
    ci = gate_c * h_b
    rb = _rms(gate_b * conv_fn(ci), g_ob_ref[...]).astype(BF16)
    merged = _dot(ra, w_out_ref[0:W_A, :]) + _dot(rb, w_out_ref[W_A:, :])
    y = x + _rms(merged, gt * gpost_ref[...])
    return y, v, ci


def _mixer_prompt_body(tiles_per_seq, x_ref, sh_ref, sc_ref, gt_ref, gpre_ref, w_in_ref, g_v_ref,
                       wsp_ref, bsp_ref, wconv_ref, g_oa_ref, g_ob_ref, w_out_ref, gpost_ref,
                       y_ref, nmix_ref, vrows_ref, oa_s, carry_s):
    i = pl.program_id(0)
    tm = x_ref.shape[0]
    first = (i % tiles_per_seq) == 0

    def spatial_fn(u, v):
        vb = v.astype(BF16)
        tril = (lax.broadcasted_iota(jnp.int32, (CHUNK, CHUNK), 0)
                >= lax.broadcasted_iota(jnp.int32, (CHUNK, CHUNK), 1))
        for hd in range(N_HEADS_A):
            cols = slice(hd * HEAD_DIM_A, (hd + 1) * HEAD_DIM_A)
            wm = jnp.where(tril, wsp_ref[hd], 0.0).astype(BF16)
            bias = bsp_ref[:, hd:hd + 1]
            for c in range(tm // CHUNK):
                rows = slice(c * CHUNK, (c + 1) * CHUNK)
                mixed = _dot(wm, vb[rows, cols]) + bias
                oa_s[rows, cols] = u[rows, cols] * mixed

    def conv_fn(ci):
        prev2 = jnp.where(first, 0.0, carry_s[6:7, :])
        prev1 = jnp.where(first, 0.0, carry_s[7:8, :])
        return _conv_rows(ci, prev2, prev1, wconv_ref, None)

    y, v, ci = _mixer_core(x_ref[...], sh_ref[...], sc_ref[...], gt_ref[...], gpre_ref, w_in_ref,
                           g_v_ref, g_oa_ref, g_ob_ref, w_out_ref, gpost_ref, oa_s,
                           spatial_fn, conv_fn)
    y_ref[...] = y
    carry_s[...] = ci[tm - 8:, :]
    nmix_ref[...] = ci[tm - 2:, :]
    vrows_ref[...] = v[tm - CHUNK:, :]


def _mixer_sample_body(layer, x_ref, sh_ref, sc_ref, gt_ref, gpre_ref, w_in_ref, g_v_ref,
                       wsp_ref, bsp_ref, wconv_ref, past0_ref, past1_ref, g_oa_ref, g_ob_ref,
                       w_out_ref, gpost_ref, y_ref, nmix_ref, vrows_ref, oa_s):
    n_t, bb, _ = x_ref.shape
    tm = n_t * bb

    def rep(m_ref):
        return jnp.concatenate([m_ref[...]] * n_t, axis=0)

    def spatial_fn(u, v):
        for hd in range(N_HEADS_A):
            cols = slice(hd * HEAD_DIM_A, (hd + 1) * HEAD_DIM_A)
            base = (layer * N_HEADS_A + hd) * n_t
            for t in range(n_t):
                mixed = None
                for s in range(t + 1):
                    term = wsp_ref[(base + t) * n_t + s] * v[s * bb:(s + 1) * bb, cols]
                    mixed = term if mixed is None else mixed + term
                mixed = mixed + bsp_ref[base + t]
                oa_s[t * bb:(t + 1) * bb, cols] = u[t * bb:(t + 1) * bb, cols] * mixed

    def conv_fn(ci):
        return _conv_blocks(ci, past0_ref[...], past1_ref[...], wconv_ref, None, bb)

    x = x_ref[...].reshape(tm, D_MODEL)
    y, v, ci = _mixer_core(x, rep(sh_ref), rep(sc_ref), rep(gt_ref), gpre_ref, w_in_ref,
                           g_v_ref, g_oa_ref, g_ob_ref, w_out_ref, gpost_ref, oa_s,
                           spatial_fn, conv_fn)
    y_ref[...] = y.reshape(n_t, bb, D_MODEL)
    nmix_ref[...] = ci[tm - 2 * bb:, :].reshape(2, bb, W_B)
    vrows_ref[...] = v.reshape(n_t, bb, W_A)


def _layer_vec(arr, l):
    c = arr.shape[-1]
    return arr.reshape(DEPTH, 1, c), pl.BlockSpec((None, 1, c), lambda *_: (l, 0, 0))


def _resident(shape, index):
    return pl.BlockSpec(shape, lambda *_: index, pipeline_mode=pl.Buffered(1))


def _mixer_prompt_call(l, x, mod_p, g_pre, w_in_bf, g_v, w_spatial, b_spatial_t, w_conv, g_oa, g_ob,
                       w_out_bf, g_post):
    n_tok = x.shape[0]
    n_seq = mod_p.shape[1]
    seq = n_tok // n_seq
    tm = MIX_ROWS
    tps = seq // tm

    def mod_spec(k):
        return pl.BlockSpec((None, None, 1, D_MODEL), lambda i: (l, i // tps, 0, k))

    gpre_a, gpre_s = _layer_vec(g_pre, l)
    gv_a, gv_s = _layer_vec(g_v, l)
    goa_a, goa_s = _layer_vec(g_oa, l)
    gob_a, gob_s = _layer_vec(g_ob, l)
    gpost_a, gpost_s = _layer_vec(g_post, l)
    return pl.pallas_call(
        functools.partial(_mixer_prompt_body, tps),
        grid=(n_tok // tm,),
        in_specs=[
            pl.BlockSpec((tm, D_MODEL), lambda i: (i, 0)),
            mod_spec(0), mod_spec(1), mod_spec(2),
            gpre_s,
            _resident((D_MODEL, IN_COLS), (0, 0)),
            gv_s,
            pl.BlockSpec((None, N_HEADS_A, CHUNK, CHUNK), lambda i: (l, 0, 0, 0)),
            pl.BlockSpec((None, CHUNK, N_HEADS_A), lambda i: (l, 0, 0)),
            pl.BlockSpec((None, 3, W_B), lambda i: (l, 0, 0)),
            goa_s, gob_s,
            _resident((D_MODEL, D_MODEL), (0, 0)),
            gpost_s,
        ],
        out_specs=[
            pl.BlockSpec((tm, D_MODEL), lambda i: (i, 0)),
            pl.BlockSpec((None, 2, W_B), lambda i: (i // tps, 0, 0)),
            pl.BlockSpec((None, CHUNK, W_A), lambda i: (i // tps, 0, 0)),
        ],
        out_shape=[
            jax.ShapeDtypeStruct((n_tok, D_MODEL), F32),
            jax.ShapeDtypeStruct((n_seq, 2, W_B), F32),
            jax.ShapeDtypeStruct((n_seq, CHUNK, W_A), F32),
        ],
        scratch_shapes=[pltpu.VMEM((tm, W_A), F32), pltpu.VMEM((8, W_B), F32)],
        compiler_params=pltpu.CompilerParams(
            dimension_semantics=("arbitrary",), vmem_limit_bytes=VMEM_LIMIT_BYTES),
        name="mixer_prompt",
    )(x, mod_p, mod_p, mod_p, gpre_a, w_in_bf, gv_a, w_spatial, b_spatial_t, w_conv, goa_a, gob_a,
      w_out_bf, gpost_a)


def _mixer_sample_call(l, x, mod, state_mix, g_pre, w_in_bf, g_v, wsp_flat, bsp_flat, w_conv, g_oa,
                       g_ob, w_out_bf, g_post):
    n_t, n_b, _ = x.shape
    bb = MIX_ROWS // n_t
    tm = n_t * bb

    def mod_spec(k):
        return pl.BlockSpec((None, bb, D_MODEL), lambda i: (l, i, k))

    def past_spec(r):
        return pl.BlockSpec((bb, W_B), lambda i: (i, l * 2 + r))

    gpre_a, gpre_s = _layer_vec(g_pre, l)
    gv_a, gv_s = _layer_vec(g_v, l)
    goa_a, goa_s = _layer_vec(g_oa, l)
    gob_a, gob_s = _layer_vec(g_ob, l)
    gpost_a, gpost_s = _layer_vec(g_post, l)
    smem = pl.BlockSpec(memory_space=pltpu.SMEM)
    return pl.pallas_call(
        functools.partial(_mixer_sample_body, l),
        grid=(n_b // bb,),
        in_specs=[
            pl.BlockSpec((n_t, bb, D_MODEL), lambda i: (0, i, 0)),
            mod_spec(0), mod_spec(1), mod_spec(2),
            gpre_s,
            _resident((D_MODEL, IN_COLS), (0, 0)),
            gv_s,
            smem, smem,
            pl.BlockSpec((None, 3, W_B), lambda i: (l, 0, 0)),
            past_spec(0), past_spec(1),
            goa_s, gob_s,
            _resident((D_MODEL, D_MODEL), (0, 0)),
            gpost_s,
        ],
        out_specs=[
            pl.BlockSpec((n_t, bb, D_MODEL), lambda i: (0, i, 0)),
            pl.BlockSpec((2, bb, W_B), lambda i: (0, i, 0)),
            pl.BlockSpec((n_t, bb, W_A), lambda i: (0, i, 0)),
        ],
        out_shape=[
            jax.ShapeDtypeStruct((n_t, n_b, D_MODEL), F32),
            jax.ShapeDtypeStruct((2, n_b, W_B), F32),
            jax.ShapeDtypeStruct((n_t, n_b, W_A), F32),
        ],
        scratch_shapes=[pltpu.VMEM((tm, W_A), F32)],
        compiler_params=pltpu.CompilerParams(
            dimension_semantics=("arbitrary",), vmem_limit_bytes=VMEM_LIMIT_BYTES),
        name="mixer_sample",
    )(x, mod, mod, mod, gpre_a, w_in_bf, gv_a, wsp_flat, bsp_flat, w_conv, state_mix, state_mix,
      goa_a, gob_a, w_out_bf, gpost_a)


def _ffn_chunk_step(tm, halo, step, wg_ref, wv_ref, cwg_ref, cwv_ref, cbg_ref, cbv_ref, wd_ref,
                    h_s, upg_s, upv_s, act_s, acc_add):
    n_k = tm // FFN_ROW_BLOCK

    def up(k):
        rows = slice(k * FFN_ROW_BLOCK, (k + 1) * FFN_ROW_BLOCK)
        dst = slice(halo + k * FFN_ROW_BLOCK, halo + (k + 1) * FFN_ROW_BLOCK)
        h = h_s[rows, :]
        upg_s[dst, :] = _dot(h, wg_ref[...])
        upv_s[dst, :] = _dot(h, wv_ref[...])

    def conv(up_s, w_ref, b_ref, r, n):
        t2, t1, t0 = [up_s[pl.ds(halo - k * step + r, n), :] for k in (2, 1, 0)]
        return (w_ref[0:1, :] * t2 + w_ref[1:2, :] * t1 + w_ref[2:3, :] * t0) + b_ref[...]

    def gate_down(k):
        for r in range(k * FFN_ROW_BLOCK, (k + 1) * FFN_ROW_BLOCK, CONV_ROWS):
            cg = conv(upg_s, cwg_ref, cbg_ref, r, CONV_ROWS)
            cv = conv(upv_s, cwv_ref, cbv_ref, r, CONV_ROWS)
            act_s[r:r + CONV_ROWS, :] = (_silu(cg) * cv).astype(BF16)
        rows = slice(k * FFN_ROW_BLOCK, (k + 1) * FFN_ROW_BLOCK)
        for c in range(0, D_MODEL, DOWN_COLS):
            acc_add(k, c, _dot(act_s[rows, :], wd_ref[:, c:c + DOWN_COLS]))

    for k in range(min(FFN_UP_AHEAD, n_k)):
        up(k)
    for k in range(n_k):
        if k + FFN_UP_AHEAD < n_k:
            up(k + FFN_UP_AHEAD)
        gate_down(k)


def _ffn_prompt_body(tiles_per_seq, has_cast, x_ref, sh_ref, sc_ref, gt_ref, gpre_ref, wgv_ref,
                     cwg_ref, cwv_ref, cbg_ref, cbv_ref, wd_ref, gpost_ref, *rest):
    n_in, n_out = (5, 4) if has_cast else (0, 0)
    cast_in, rest = rest[:n_in], rest[n_in:]
    o_ref, nfg_ref, nfv_ref = rest[:3]
    cast_out, rest = rest[3:3 + n_out], rest[3 + n_out:]
    h_s, upg_s, upv_s, act_s, carry_g, carry_v = rest
    fc = wd_ref.shape[0]
    wg_ref, wv_ref = wgv_ref.at[:, 0:fc], wgv_ref.at[:, fc:2 * fc]
    i = pl.program_id(0)
    j = pl.program_id(1)
    n_j = pl.num_programs(1)
    tm = x_ref.shape[0]
    halo = 8
    first_tile = (i % tiles_per_seq) == 0

    if has_cast:
        up_g, up_v, *plain_in = cast_in
        up_dst, *plain_out = cast_out
        up_dst[:, 0:fc] = up_g[...].astype(BF16)
        up_dst[:, fc:2 * fc] = up_v[...].astype(BF16)
        for src, dst in zip(plain_in, plain_out):
            dst[...] = src[...].astype(BF16)

    @pl.when(j == 0)
    def _():
        gain = gpre_ref[...] * (1.0 + sc_ref[...])
        for r in range(0, tm, NORM_ROWS):
            rows = slice(r, r + NORM_ROWS)
            h_s[rows, :] = (_rms(x_ref[rows, :], gain) + sh_ref[...]).astype(BF16)
        o_ref[...] = jnp.zeros_like(o_ref)

    upg_s[0:halo, :] = jnp.where(first_tile, 0.0, carry_g[j])
    upv_s[0:halo, :] = jnp.where(first_tile, 0.0, carry_v[j])

    def acc_add(k, c, part):
        rows = slice(k * FFN_ROW_BLOCK, (k + 1) * FFN_ROW_BLOCK)
        o_ref[rows, c:c + part.shape[1]] += part

    _ffn_chunk_step(tm, halo, 1, wg_ref, wv_ref, cwg_ref, cwv_ref, cbg_ref, cbv_ref, wd_ref,
                    h_s, upg_s, upv_s, act_s, acc_add)
    carry_g[j] = upg_s[tm:tm + halo, :]
    carry_v[j] = upv_s[tm:tm + halo, :]
    nfg_ref[...] = upg_s[halo + tm - 2:halo + tm, :]
    nfv_ref[...] = upv_s[halo + tm - 2:halo + tm, :]

    @pl.when(j == n_j - 1)
    def _():
        gain = gt_ref[...] * gpost_ref[...]
        for r in range(0, tm, NORM_ROWS):
            rows = slice(r, r + NORM_ROWS)
            o_ref[rows, :] = x_ref[rows, :] + _rms(o_ref[rows, :], gain)


def _ffn_sample_body(x_ref, sh_ref, sc_ref, gt_ref, gpre_ref, wgv_ref,
                     cwg_ref, cwv_ref, cbg_ref, cbv_ref, pg_ref, pv_ref,
                     wd_ref, gpost_ref, o_ref, nfg_ref, nfv_ref, h_s, upg_s, upv_s, act_s, acc_s):
    fc = wd_ref.shape[0]
    wg_ref, wv_ref = wgv_ref.at[:, 0:fc], wgv_ref.at[:, fc:2 * fc]
    j = pl.program_id(1)
    n_j = pl.num_programs(1)
    n_t, bb, _ = x_ref.shape
    tm = n_t * bb
    halo = 2 * bb

    def passes():
        for t in range(n_t):
            for r in range(0, bb, NORM_ROWS):
                yield t, slice(r, r + NORM_ROWS), slice(t * bb + r, t * bb + r + NORM_ROWS)

    @pl.when(j == 0)
    def _():
        for t, rb, rows in passes():
            h_s[rows, :] = (_rms(x_ref[t, rb, :], gpre_ref[...]) * (1.0 + sc_ref[rb, :])
                            + sh_ref[rb, :]).astype(BF16)
        acc_s[...] = jnp.zeros_like(acc_s)

    for r in range(2):
        upg_s[r * bb:(r + 1) * bb, :] = pg_ref[:, r, :]
        upv_s[r * bb:(r + 1) * bb, :] = pv_ref[:, r, :]

    def acc_add(k, c, part):
        rows = slice(k * FFN_ROW_BLOCK, (k + 1) * FFN_ROW_BLOCK)
        acc_s[rows, c:c + part.shape[1]] += part

    _ffn_chunk_step(tm, halo, bb, wg_ref, wv_ref, cwg_ref, cwv_ref, cbg_ref, cbv_ref, wd_ref,
                    h_s, upg_s, upv_s, act_s, acc_add)
    for k in range(2):
        rows = slice(tm + k * bb, tm + (k + 1) * bb)
        nfg_ref[k] = upg_s[rows, :]
        nfv_ref[k] = upv_s[rows, :]

    @pl.when(j == n_j - 1)
    def _():
        for t, rb, rows in passes():
            o_ref[t, rb, :] = x_ref[t, rb, :] + gt_ref[rb, :] * _rms(acc_s[rows, :], gpost_ref[...])


def _ffn_weight_specs(l, fc):
    n_j = D_FF // fc
    return [
        pl.BlockSpec((None, D_MODEL, 2 * fc), lambda i, j: (j, 0, 0)),
        pl.BlockSpec((None, 3, fc), lambda i, j: (l, 0, j)),
        pl.BlockSpec((None, 3, fc), lambda i, j: (l, 0, n_j + j)),
        pl.BlockSpec((None, 1, fc), lambda i, j: (l, 0, j)),
        pl.BlockSpec((None, 1, fc), lambda i, j: (l, 0, n_j + j)),
    ]


def _next_layer_cast_specs(l, n_tiles, n_j, weights):
    w_up, w_down, w_in, w_out = weights
    rows = D_MODEL // n_tiles
    fc = D_FF // n_j
    down_rows = D_FF // (n_tiles * n_j)
    in_blocks, out_blocks = CAST_IN_BLOCKS, CAST_OUT_BLOCKS
    assert in_blocks <= n_j and out_blocks <= n_j and down_rows * n_tiles * n_j == D_FF

    def layer(idx):
        return lambda i, j: (l + 1,) + idx(i, j)

    def in_idx(i, j):
        return (i, jnp.minimum(j, in_blocks - 1))

    def out_idx(i, j):
        return (i, jnp.minimum(j, out_blocks - 1))

    in_specs = [
        pl.BlockSpec((None, rows, fc), layer(lambda i, j: (i, j))),
        pl.BlockSpec((None, rows, fc), layer(lambda i, j: (i, n_j + j))),
        pl.BlockSpec((None, down_rows, D_MODEL), layer(lambda i, j: (i * n_j + j, 0))),
        pl.BlockSpec((None, rows, IN_COLS // in_blocks), layer(in_idx)),
        pl.BlockSpec((None, rows, D_MODEL // out_blocks), layer(out_idx)),
    ]
    out_specs = [
        pl.BlockSpec((None, rows, 2 * fc), lambda i, j: (j, i, 0)),
        pl.BlockSpec((down_rows, D_MODEL), lambda i, j: (i * n_j + j, 0)),
        pl.BlockSpec((rows, IN_COLS // in_blocks), in_idx),
        pl.BlockSpec((rows, D_MODEL // out_blocks), out_idx),
    ]
    out_shape = [
        jax.ShapeDtypeStruct((n_j, D_MODEL, 2 * fc), BF16),
        jax.ShapeDtypeStruct(w_down.shape[1:], BF16),
        jax.ShapeDtypeStruct(w_in.shape[1:], BF16),
        jax.ShapeDtypeStruct(w_out.shape[1:], BF16),
    ]
    return in_specs, out_specs, out_shape


def _ffn_prompt_call(l, x, mod_p, g_pre, w_up_bf, w_conv, b_conv, w_down_bf, g_post, next_weights):
    n_tok = x.shape[0]
    n_seq = mod_p.shape[1]
    seq = n_tok // n_seq
    tm, fc = FFN_ROWS, FFN_COLS
    tps = seq // tm
    n_j = D_FF // fc
    halo = 8
    cast_in, cast_out, cast_shape, cast_args = [], [], [], ()
    if next_weights is not None:
        cast_in, cast_out, cast_shape = _next_layer_cast_specs(l, n_tok // tm, n_j, next_weights)
        cast_args = (next_weights[0],) + tuple(next_weights)

    def mod_spec(k):
        return pl.BlockSpec((None, None, 1, D_MODEL), lambda i, j: (l, i // tps, 0, k))

    gpre_a, gpre_s = _layer_vec(g_pre, l)
    gpost_a, gpost_s = _layer_vec(g_post, l)
    b3 = b_conv.reshape(DEPTH, 1, 2 * D_FF)
    outs = pl.pallas_call(
        functools.partial(_ffn_prompt_body, tps, next_weights is not None),
        grid=(n_tok // tm, n_j),
        in_specs=[
            pl.BlockSpec((tm, D_MODEL), lambda i, j: (i, 0)),
            mod_spec(3), mod_spec(4), mod_spec(5),
            gpre_s,
            *_ffn_weight_specs(l, fc),
            pl.BlockSpec((fc, D_MODEL), lambda i, j: (j, 0)),
            gpost_s,
            *cast_in,
        ],
        out_specs=[
            pl.BlockSpec((tm, D_MODEL), lambda i, j: (i, 0)),
            pl.BlockSpec((None, 2, fc), lambda i, j: (i, 0, j)),
            pl.BlockSpec((None, 2, fc), lambda i, j: (i, 0, j)),
            *cast_out,
        ],
        out_shape=[
            jax.ShapeDtypeStruct((n_tok, D_MODEL), F32),
            jax.ShapeDtypeStruct((n_tok // tm, 2, D_FF), F32),
            jax.ShapeDtypeStruct((n_tok // tm, 2, D_FF), F32),
            *cast_shape,
        ],
        scratch_shapes=[
            pltpu.VMEM((tm, D_MODEL), BF16),
            pltpu.VMEM((halo + tm, fc), F32),
            pltpu.VMEM((halo + tm, fc), F32),
            pltpu.VMEM((tm, fc), BF16),
            pltpu.VMEM((n_j, halo, fc), F32),
            pltpu.VMEM((n_j, halo, fc), F32),
        ],
        compiler_params=pltpu.CompilerParams(
            dimension_semantics=("arbitrary", "arbitrary"), vmem_limit_bytes=VMEM_LIMIT_BYTES),
        name="ffn_prompt",
    )(x, mod_p, mod_p, mod_p, gpre_a, w_up_bf, w_conv, w_conv, b3, b3, w_down_bf, gpost_a,
      *cast_args)
    return outs[0], outs[1], outs[2], tuple(outs[3:])


def _ffn_sample_call(l, x, mod, state_ffn, g_pre, w_up_bf, w_conv, b_conv, w_down_bf, g_post):
    n_t, n_b, _ = x.shape
    fc = FFN_COLS
    bb = FFN_ROWS // n_t
    tm = n_t * bb
    n_j = D_FF // fc

    def mod_spec(k):
        return pl.BlockSpec((None, bb, D_MODEL), lambda i, j: (l, i, k))

    def past_spec(half):
        return pl.BlockSpec((bb, None, 2, fc), lambda i, j: (i, l, 0, half * n_j + j))

    gpre_a, gpre_s = _layer_vec(g_pre, l)
    gpost_a, gpost_s = _layer_vec(g_post, l)
    b3 = b_conv.reshape(DEPTH, 1, 2 * D_FF)
    return pl.pallas_call(
        _ffn_sample_body,
        grid=(n_b // bb, n_j),
        in_specs=[
            pl.BlockSpec((n_t, bb, D_MODEL), lambda i, j: (0, i, 0)),
            mod_spec(3), mod_spec(4), mod_spec(5),
            gpre_s,
            *_ffn_weight_specs(l, fc),
            past_spec(0), past_spec(1),
            pl.BlockSpec((fc, D_MODEL), lambda i, j: (j, 0)),
            gpost_s,
        ],
        out_specs=[
            pl.BlockSpec((n_t, bb, D_MODEL), lambda i, j: (0, i, 0)),
            pl.BlockSpec((2, bb, fc), lambda i, j: (0, i, j)),
            pl.BlockSpec((2, bb, fc), lambda i, j: (0, i, j)),
        ],
        out_shape=[
            jax.ShapeDtypeStruct((n_t, n_b, D_MODEL), F32),
            jax.ShapeDtypeStruct((2, n_b, D_FF), F32),
            jax.ShapeDtypeStruct((2, n_b, D_FF), F32),
        ],
        scratch_shapes=[
            pltpu.VMEM((tm, D_MODEL), BF16),
            pltpu.VMEM((2 * bb + tm, fc), F32),
            pltpu.VMEM((2 * bb + tm, fc), F32),
            pltpu.VMEM((tm, fc), BF16),
            pltpu.VMEM((tm, D_MODEL), F32),
        ],
        compiler_params=pltpu.CompilerParams(
            dimension_semantics=("arbitrary", "arbitrary"), vmem_limit_bytes=VMEM_LIMIT_BYTES),
        name="ffn_sample",
    )(x, mod, mod, mod, gpre_a, w_up_bf, w_conv, w_conv, b3, b3,
      state_ffn, state_ffn, w_down_bf, gpost_a)


def kernel(x_prompt, x_sample, state_conv_mix, state_conv_ffn, c_prompt, c_sample, w_ada, b_ada, g_pre_mix, g_post_mix, g_pre_ffn, g_post_ffn, w_in, g_v, w_spatial, b_spatial, w_conv_mix, g_out_a, g_out_b, w_out, w_up, w_conv_ffn, b_conv_ffn, w_down):
    n_seq, seq, _ = x_prompt.shape
    n_b, n_t, _ = x_sample.shape

    pad = (-(n_b + n_seq)) % 8
    c_all = jnp.concatenate([c_sample, c_prompt, jnp.zeros((pad, D_MODEL), F32)], axis=0)
    mod = _ada_call(c_all, w_ada, b_ada)
    mod_p = mod[:, n_b:n_b + n_seq].reshape(DEPTH, n_seq, 1, N_MOD * D_MODEL)

    w_up_bf, w_down_bf, w_in_bf, w_out_bf = [w[0].astype(BF16) for w in (w_up, w_down, w_in, w_out)]
    n_j = D_FF // FFN_COLS
    w_up_bf = (w_up_bf.reshape(D_MODEL, 2, n_j, FFN_COLS).transpose(2, 0, 1, 3)
               .reshape(n_j, D_MODEL, 2 * FFN_COLS))
    b_spatial_t = jnp.swapaxes(b_spatial, 1, 2)
    wsp_flat = w_spatial[:, :, :n_t, :n_t].reshape(-1)
    bsp_flat = b_spatial[:, :, :n_t].reshape(-1)
    state_mix = state_conv_mix.reshape(n_b, DEPTH * 2 * W_B)

    xp = x_prompt.reshape(n_seq * seq, D_MODEL)
    xs = jnp.swapaxes(x_sample, 0, 1)
    mix_p, mix_s, ffn_p, ffn_s, vr_p, vr_s = [], [], [], [], [], []
    for l in range(DEPTH):
        xp, nm, vr = _mixer_prompt_call(l, xp, mod_p, g_pre_mix, w_in_bf, g_v, w_spatial,
                                        b_spatial_t, w_conv_mix, g_out_a, g_out_b, w_out_bf,
                                        g_post_mix)
        mix_p.append(nm)
        vr_p.append(vr)
        xs, nm, vr = _mixer_sample_call(l, xs, mod, state_mix, g_pre_mix, w_in_bf, g_v, wsp_flat,
                                        bsp_flat, w_conv_mix, g_out_a, g_out_b, w_out_bf,
                                        g_post_mix)
        mix_s.append(jnp.swapaxes(nm, 0, 1))
        vr_s.append(jnp.swapaxes(vr, 0, 1))
        xs, nfg, nfv = _ffn_sample_call(l, xs, mod, state_conv_ffn, g_pre_ffn, w_up_bf, w_conv_ffn,
                                        b_conv_ffn, w_down_bf, g_post_ffn)
        ffn_s.append(jnp.swapaxes(jnp.concatenate([nfg, nfv], axis=-1), 0, 1))
        next_weights = (w_up, w_down, w_in, w_out) if l + 1 < DEPTH else None
        xp, nfg, nfv, casts = _ffn_prompt_call(l, xp, mod_p, g_pre_ffn, w_up_bf, w_conv_ffn,
                                               b_conv_ffn, w_down_bf, g_post_ffn, next_weights)
        tps = nfg.shape[0] // n_seq
        ffn_p.append(jnp.concatenate([nfg, nfv], axis=-1)[tps - 1::tps])
        if casts:
            w_up_bf, w_down_bf, w_in_bf, w_out_bf = casts

    return (xp.reshape(n_seq, seq, D_MODEL), jnp.swapaxes(xs, 0, 1),
            jnp.stack(mix_p, axis=1), jnp.stack(mix_s, axis=1),
            jnp.stack(ffn_p, axis=1), jnp.stack(ffn_s, axis=1),
            jnp.stack(vr_p, axis=1), jnp.stack(vr_s, axis=1))
```

```python
import functools

import jax
import jax.numpy as jnp
from jax import lax
from jax.experimental import pallas as pl
from jax.experimental.pallas import tpu as pltpu

F32 = jnp.float32
BF16 = jnp.bfloat16

D_MODEL = 2048
DEPTH = 4
CHUNK = 128
N_HEADS_A = 8
HEAD_DIM_A = 128
W_A = N_HEADS_A * HEAD_DIM_A
W_B = D_MODEL - W_A
D_FF = 5632
N_MOD = 6
IN_COLS = 2 * W_A + 3 * W_B
EPS = 1e-6

MIX_ROWS = 256
FFN_ROWS = 1024
FFN_SAMPLE_ROWS = 512
FFN_COLS = 512
NORM_ROWS = 16
CONV_ROWS = 64
FFN_ROW_BLOCK = 256
FFN_UP_AHEAD = 1
DOWN_COLS = 2048
CAST_IN_BLOCKS = 10
CAST_OUT_BLOCKS = 8
ADA_COLS = 1024
VMEM_LIMIT_BYTES = 56 * 1024 * 1024
FFN_PROMPT_VMEM_LIMIT_BYTES = 60 * 1024 * 1024


def _rms(x, g):
    ms = jnp.mean(x * x, axis=-1, keepdims=True)
    return x * lax.rsqrt(ms + EPS) * g


def _gelu(x):
    c = 0.7978845608028654
    return 0.5 * x * (1.0 + jnp.tanh(c * (x + 0.044715 * (x * x * x))))


def _silu(x):
    h = 0.5 * x
    return h + h * jnp.tanh(h)


def _dot(a, b):
    return jnp.dot(a, b, preferred_element_type=F32)


def _conv_rows(u, prev2, prev1, w_ref, b_ref):
    rows = lax.broadcasted_iota(jnp.int32, u.shape, 0)
    p1 = jnp.where(rows == 0, prev1, pltpu.roll(u, 1, 0))
    p2 = jnp.where(rows == 0, prev2, jnp.where(rows == 1, prev1, pltpu.roll(u, 2, 0)))
    y = w_ref[0:1, :] * p2 + w_ref[1:2, :] * p1 + w_ref[2:3, :] * u
    if b_ref is not None:
        y = y + b_ref[...]
    return y


def _conv_blocks(u, past0, past1, w_ref, b_ref, bb):
    t = u.shape[0]
    p1 = jnp.concatenate([past1, u[: t - bb]], axis=0)
    p2 = jnp.concatenate([past0, past1, u[: t - 2 * bb]], axis=0)
    y = w_ref[0:1, :] * p2 + w_ref[1:2, :] * p1 + w_ref[2:3, :] * u
    if b_ref is not None:
        y = y + b_ref[...]
    return y


def _ada_body(c_ref, w_ref, b_ref, o_ref):
    s = _silu(c_ref[...]).astype(BF16)
    o_ref[...] = _dot(s, w_ref[...].astype(BF16)) + b_ref[...]


def _ada_call(c_all, w_ada, b_ada):
    rows = c_all.shape[0]
    n_col = (N_MOD * D_MODEL) // ADA_COLS
    return pl.pallas_call(
        _ada_body,
        grid=(DEPTH, n_col),
        in_specs=[
            pl.BlockSpec((rows, D_MODEL), lambda l, n: (0, 0)),
            pl.BlockSpec((None, D_MODEL, ADA_COLS), lambda l, n: (l, 0, n)),
            pl.BlockSpec((None, 1, ADA_COLS), lambda l, n: (l, 0, n)),
        ],
        out_specs=pl.BlockSpec((None, rows, ADA_COLS), lambda l, n: (l, 0, n)),
        out_shape=jax.ShapeDtypeStruct((DEPTH, rows, N_MOD * D_MODEL), F32),
        compiler_params=pltpu.CompilerParams(
            dimension_semantics=("arbitrary", "arbitrary"),
            vmem_limit_bytes=VMEM_LIMIT_BYTES),
        name="ada_mod",
    )(c_all, w_ada, b_ada.reshape(DEPTH, 1, N_MOD * D_MODEL))


def _mixer_core(x, sh, sc, gt, gpre_ref, w_in_ref, g_v_ref, g_oa_ref, g_ob_ref, w_out_ref,
                gpost_ref, oa_s, spatial_fn, conv_fn):
    h = (_rms(x, gpre_ref[...] * (1.0 + sc)) + sh).astype(BF16)
    u = _gelu(_dot(h, w_in_ref[:, 0:W_A]))
    v = _rms(_gelu(_dot(h, w_in_ref[:, W_A:2 * W_A])), g_v_ref[...])
    spatial_fn(u, v)
    ra = _rms(oa_s[...], g_oa_ref[...]).astype(BF16)
    gate_b = _dot(h, w_in_ref[:, 2 * W_A:2 * W_A + W_B])
    gate_c = _dot(h, w_in_ref[:, 2 * W_A + W_B:2 * W_A + 2 * W_B])
    h_b = _dot(h, w_in_ref[:, 2 * W_A + 2 * W_B:])
    ci = gate_c * h_b
    rb = _rms(gate_b * conv_fn(ci), g_ob_ref[...]).astype(BF16)
    merged = _dot(ra, w_out_ref[0:W_A, :]) + _dot(rb, w_out_ref[W_A:, :])
    y = x + _rms(merged, gt * gpost_ref[...])
    return y, v, ci


def _mixer_prompt_body(tiles_per_seq, x_ref, sh_ref, sc_ref, gt_ref, gpre_ref, w_in_ref, g_v_ref,
                       wsp_ref, bsp_ref, wconv_ref, g_oa_ref, g_ob_ref, w_out_ref, gpost_ref,
                       y_ref, nmix_ref, vrows_ref, oa_s, carry_s):
    i = pl.program_id(0)
    tm = x_ref.shape[0]
    first = (i % tiles_per_seq) == 0

    def spatial_fn(u, v):
        vb = v.astype(BF16)
        tril = (lax.broadcasted_iota(jnp.int32, (CHUNK, CHUNK), 0)
                >= lax.broadcasted_iota(jnp.int32, (CHUNK, CHUNK), 1))
        for hd in range(N_HEADS_A):
            cols = slice(hd * HEAD_DIM_A, (hd + 1) * HEAD_DIM_A)
            wm = jnp.where(tril, wsp_ref[hd], 0.0).astype(BF16)
            bias = bsp_ref[:, hd:hd + 1]
            for c in range(tm // CHUNK):
                rows = slice(c * CHUNK, (c + 1) * CHUNK)
                mixed = _dot(wm, vb[rows, cols]) + bias
                oa_s[rows, cols] = u[rows, cols] * mixed

    def conv_fn(ci):
        prev2 = jnp.where(first, 0.0, carry_s[6:7, :])
        prev1 = jnp.where(first, 0.0, carry_s[7:8, :])
        return _conv_rows(ci, prev2, prev1, wconv_ref, None)

    y, v, ci = _mixer_core(x_ref[...], sh_ref[...], sc_ref[...], gt_ref[...], gpre_ref, w_in_ref,
                           g_v_ref, g_oa_ref, g_ob_ref, w_out_ref, gpost_ref, oa_s,
                           spatial_fn, conv_fn)
    y_ref[...] = y
    carry_s[...] = ci[tm - 8:, :]
    nmix_ref[...] = ci[tm - 2:, :]
    vrows_ref[...] = v[tm - CHUNK:, :]


def _mixer_sample_body(layer, x_ref, sh_ref, sc_ref, gt_ref, gpre_ref, w_in_ref, g_v_ref,
                       wsp_ref, bsp_ref, wconv_ref, past0_ref, past1_ref, g_oa_ref, g_ob_ref,
                       w_out_ref, gpost_ref, y_ref, nmix_ref, vrows_ref, oa_s):
    n_t, bb, _ = x_ref.shape
    tm = n_t * bb

    def rep(m_ref):
        return jnp.concatenate([m_ref[...]] * n_t, axis=0)

    def spatial_fn(u, v):
        for hd in range(N_HEADS_A):
            cols = slice(hd * HEAD_DIM_A, (hd + 1) * HEAD_DIM_A)
            base = (layer * N_HEADS_A + hd) * n_t
            for t in range(n_t):
                mixed = None
                for s in range(t + 1):
                    term = wsp_ref[(base + t) * n_t + s] * v[s * bb:(s + 1) * bb, cols]
                    mixed = term if mixed is None else mixed + term
                mixed = mixed + bsp_ref[base + t]
                oa_s[t * bb:(t + 1) * bb, cols] = u[t * bb:(t + 1) * bb, cols] * mixed

    def conv_fn(ci):
        return _conv_blocks(ci, past0_ref[...], past1_ref[...], wconv_ref, None, bb)

    x = x_ref[...].reshape(tm, D_MODEL)
    y, v, ci = _mixer_core(x, rep(sh_ref), rep(sc_ref), rep(gt_ref), gpre_ref, w_in_ref,
                           g_v_ref, g_oa_ref, g_ob_ref, w_out_ref, gpost_ref, oa_s,
                           spatial_fn, conv_fn)
    y_ref[...] = y.reshape(n_t, bb, D_MODEL)
    nmix_ref[...] = ci[tm - 2 * bb:, :].reshape(2, bb, W_B)
    vrows_ref[...] = v.reshape(n_t, bb, W_A)


def _layer_vec(arr, l):
    c = arr.shape[-1]
    return arr.reshape(DEPTH, 1, c), pl.BlockSpec((None, 1, c), lambda *_: (l, 0, 0))


def _resident(shape, index):
    return pl.BlockSpec(shape, lambda *_: index, pipeline_mode=pl.Buffered(1))


def _mixer_prompt_call(l, x, mod_p, g_pre, w_in_bf, g_v, w_spatial, b_spatial_t, w_conv, g_oa, g_ob,
                       w_out_bf, g_post):
    n_tok = x.shape[0]
    n_seq = mod_p.shape[1]
    seq = n_tok // n_seq
    tm = MIX_ROWS
    tps = seq // tm

    def mod_spec(k):
        return pl.BlockSpec((None, None, 1, D_MODEL), lambda i: (l, i // tps, 0, k))

    gpre_a, gpre_s = _layer_vec(g_pre, l)
    gv_a, gv_s = _layer_vec(g_v, l)
    goa_a, goa_s = _layer_vec(g_oa, l)
    gob_a, gob_s = _layer_vec(g_ob, l)
    gpost_a, gpost_s = _layer_vec(g_post, l)
    return pl.pallas_call(
        functools.partial(_mixer_prompt_body, tps),
        grid=(n_tok // tm,),
        in_specs=[
            pl.BlockSpec((tm, D_MODEL), lambda i: (i, 0)),
            mod_spec(0), mod_spec(1), mod_spec(2),
            gpre_s,
            _resident((D_MODEL, IN_COLS), (0, 0)),
            gv_s,
            pl.BlockSpec((None, N_HEADS_A, CHUNK, CHUNK), lambda i: (l, 0, 0, 0)),
            pl.BlockSpec((None, CHUNK, N_HEADS_A), lambda i: (l, 0, 0)),
            pl.BlockSpec((None, 3, W_B), lambda i: (l, 0, 0)),
            goa_s, gob_s,
            _resident((D_MODEL, D_MODEL), (0, 0)),
            gpost_s,
        ],
        out_specs=[
            pl.BlockSpec((tm, D_MODEL), lambda i: (i, 0)),
            pl.BlockSpec((None, 2, W_B), lambda i: (i // tps, 0, 0)),
            pl.BlockSpec((None, CHUNK, W_A), lambda i: (i // tps, 0, 0)),
        ],
        out_shape=[
            jax.ShapeDtypeStruct((n_tok, D_MODEL), F32),
            jax.ShapeDtypeStruct((n_seq, 2, W_B), F32),
            jax.ShapeDtypeStruct((n_seq, CHUNK, W_A), F32),
        ],
        scratch_shapes=[pltpu.VMEM((tm, W_A), F32), pltpu.VMEM((8, W_B), F32)],
        compiler_params=pltpu.CompilerParams(
            dimension_semantics=("arbitrary",), vmem_limit_bytes=VMEM_LIMIT_BYTES),
        name="mixer_prompt",
    )(x, mod_p, mod_p, mod_p, gpre_a, w_in_bf, gv_a, w_spatial, b_spatial_t, w_conv, goa_a, gob_a,
      w_out_bf, gpost_a)


def _mixer_sample_call(l, x, mod, state_mix, g_pre, w_in_bf, g_v, wsp_flat, bsp_flat, w_conv, g_oa,
                       g_ob, w_out_bf, g_post):
    n_t, n_b, _ = x.shape
    bb = MIX_ROWS // n_t
    tm = n_t * bb

    def mod_spec(k):
        return pl.BlockSpec((None, bb, D_MODEL), lambda i: (l, i, k))

    def past_spec(r):
        return pl.BlockSpec((bb, W_B), lambda i: (i, l * 2 + r))

    gpre_a, gpre_s = _layer_vec(g_pre, l)
    gv_a, gv_s = _layer_vec(g_v, l)
    goa_a, goa_s = _layer_vec(g_oa, l)
    gob_a, gob_s = _layer_vec(g_ob, l)
    gpost_a, gpost_s = _layer_vec(g_post, l)
    smem = pl.BlockSpec(memory_space=pltpu.SMEM)
    return pl.pallas_call(
        functools.partial(_mixer_sample_body, l),
        grid=(n_b // bb,),
        in_specs=[
            pl.BlockSpec((n_t, bb, D_MODEL), lambda i: (0, i, 0)),
            mod_spec(0), mod_spec(1), mod_spec(2),
            gpre_s,
            _resident((D_MODEL, IN_COLS), (0, 0)),
            gv_s,
            smem, smem,
            pl.BlockSpec((None, 3, W_B), lambda i: (l, 0, 0)),
            past_spec(0), past_spec(1),
            goa_s, gob_s,
            _resident((D_MODEL, D_MODEL), (0, 0)),
            gpost_s,
        ],
        out_specs=[
            pl.BlockSpec((n_t, bb, D_MODEL), lambda i: (0, i, 0)),
            pl.BlockSpec((2, bb, W_B), lambda i: (0, i, 0)),
            pl.BlockSpec((n_t, bb, W_A), lambda i: (0, i, 0)),
        ],
        out_shape=[
            jax.ShapeDtypeStruct((n_t, n_b, D_MODEL), F32),
            jax.ShapeDtypeStruct((2, n_b, W_B), F32),
            jax.ShapeDtypeStruct((n_t, n_b, W_A), F32),
        ],
        scratch_shapes=[pltpu.VMEM((tm, W_A), F32)],
        compiler_params=pltpu.CompilerParams(
            dimension_semantics=("arbitrary",), vmem_limit_bytes=VMEM_LIMIT_BYTES),
        name="mixer_sample",
    )(x, mod, mod, mod, gpre_a, w_in_bf, gv_a, wsp_flat, bsp_flat, w_conv, state_mix, state_mix,
      goa_a, gob_a, w_out_bf, gpost_a)


def _ffn_chunk_step(tm, halo, step, wg_ref, wv_ref, cwg_ref, cwv_ref, cbg_ref, cbv_ref, wd_ref,
                    h_s, upg_s, upv_s, act_s, acc_add):
    n_k = tm // FFN_ROW_BLOCK

    def up(k):
        rows = slice(k * FFN_ROW_BLOCK, (k + 1) * FFN_ROW_BLOCK)
        dst = slice(halo + k * FFN_ROW_BLOCK, halo + (k + 1) * FFN_ROW_BLOCK)
        h = h_s[rows, :]
        upg_s[dst, :] = _dot(h, wg_ref[...])
        upv_s[dst, :] = _dot(h, wv_ref[...])

    def conv(up_s, w_ref, b_ref, r, n):
        t2, t1, t0 = [up_s[pl.ds(halo - k * step + r, n), :] for k in (2, 1, 0)]
        return (w_ref[0:1, :] * t2 + w_ref[1:2, :] * t1 + w_ref[2:3, :] * t0) + b_ref[...]

    def gate_down(k):
        for r in range(k * FFN_ROW_BLOCK, (k + 1) * FFN_ROW_BLOCK, CONV_ROWS):
            cg = conv(upg_s, cwg_ref, cbg_ref, r, CONV_ROWS)
            cv = conv(upv_s, cwv_ref, cbv_ref, r, CONV_ROWS)
            act_s[r:r + CONV_ROWS, :] = (_silu(cg) * cv).astype(BF16)
        rows = slice(k * FFN_ROW_BLOCK, (k + 1) * FFN_ROW_BLOCK)
        for c in range(0, D_MODEL, DOWN_COLS):
            acc_add(k, c, _dot(act_s[rows, :], wd_ref[:, c:c + DOWN_COLS]))

    for k in range(min(FFN_UP_AHEAD, n_k)):
        up(k)
    for k in range(n_k):
        if k + FFN_UP_AHEAD < n_k:
            up(k + FFN_UP_AHEAD)
        gate_down(k)


def _ffn_prompt_body(tiles_per_seq, n_cast, x_ref, sh_ref, sc_ref, gt_ref, gpre_ref, wg_ref, wv_ref,
                     cwg_ref, cwv_ref, cbg_ref, cbv_ref, wd_ref, gpost_ref, *rest):
    cast_in, rest = rest[:n_cast], rest[n_cast:]
    o_ref, nfg_ref, nfv_ref = rest[:3]
    cast_out, rest = rest[3:3 + n_cast], rest[3 + n_cast:]
    h_s, upg_s, upv_s, act_s, carry_g, carry_v = rest
    i = pl.program_id(0)
    j = pl.program_id(1)
    n_j = pl.num_programs(1)
    tm = x_ref.shape[0]
    halo = 8
    first_tile = (i % tiles_per_seq) == 0

    for src, dst in zip(cast_in, cast_out):
        dst[...] = src[...].astype(BF16)

    @pl.when(j == 0)
    def _():
        gain = gpre_ref[...] * (1.0 + sc_ref[...])
        for r in range(0, tm, NORM_ROWS):
            rows = slice(r, r + NORM_ROWS)
            h_s[rows, :] = (_rms(x_ref[rows, :], gain) + sh_ref[...]).astype(BF16)
        o_ref[...] = jnp.zeros_like(o_ref)

    upg_s[0:halo, :] = jnp.where(first_tile, 0.0, carry_g[j])
    upv_s[0:halo, :] = jnp.where(first_tile, 0.0, carry_v[j])

    def acc_add(k, c, part):
        rows = slice(k * FFN_ROW_BLOCK, (k + 1) * FFN_ROW_BLOCK)
        o_ref[rows, c:c + part.shape[1]] += part

    _ffn_chunk_step(tm, halo, 1, wg_ref, wv_ref, cwg_ref, cwv_ref, cbg_ref, cbv_ref, wd_ref,
                    h_s, upg_s, upv_s, act_s, acc_add)
    carry_g[j] = upg_s[tm:tm + halo, :]
    carry_v[j] = upv_s[tm:tm + halo, :]
    nfg_ref[...] = upg_s[halo + tm - 2:halo + tm, :]
    nfv_ref[...] = upv_s[halo + tm - 2:halo + tm, :]

    @pl.when(j == n_j - 1)
    def _():
        gain = gt_ref[...] * gpost_ref[...]
        for r in range(0, tm, NORM_ROWS):
            rows = slice(r, r + NORM_ROWS)
            o_ref[rows, :] = x_ref[rows, :] + _rms(o_ref[rows, :], gain)


def _ffn_sample_body(x_ref, sh_ref, sc_ref, gt_ref, gpre_ref, wg_ref, wv_ref,
                     cwg_ref, cwv_ref, cbg_ref, cbv_ref, pg_ref, pv_ref,
                     wd_ref, gpost_ref, o_ref, nfg_ref, nfv_ref, h_s, upg_s, upv_s, act_s, acc_s):
    j = pl.program_id(1)
    n_j = pl.num_programs(1)
    n_t, bb, _ = x_ref.shape
    tm = n_t * bb
    halo = 2 * bb

    def passes():
        for t in range(n_t):
            for r in range(0, bb, NORM_ROWS):
                yield t, slice(r, r + NORM_ROWS), slice(t * bb + r, t * bb + r + NORM_ROWS)

    @pl.when(j == 0)
    def _():
        for t, rb, rows in passes():
            h_s[rows, :] = (_rms(x_ref[t, rb, :], gpre_ref[...]) * (1.0 + sc_ref[rb, :])
                            + sh_ref[rb, :]).astype(BF16)
        acc_s[...] = jnp.zeros_like(acc_s)

    for r in range(2):
        upg_s[r * bb:(r + 1) * bb, :] = pg_ref[:, r, :]
        upv_s[r * bb:(r + 1) * bb, :] = pv_ref[:, r, :]

    def acc_add(k, c, part):
        rows = slice(k * FFN_ROW_BLOCK, (k + 1) * FFN_ROW_BLOCK)
        acc_s[rows, c:c + part.shape[1]] += part

    _ffn_chunk_step(tm, halo, bb, wg_ref, wv_ref, cwg_ref, cwv_ref, cbg_ref, cbv_ref, wd_ref,
                    h_s, upg_s, upv_s, act_s, acc_add)
    for k in range(2):
        rows = slice(tm + k * bb, tm + (k + 1) * bb)
        nfg_ref[k] = upg_s[rows, :]
        nfv_ref[k] = upv_s[rows, :]

    @pl.when(j == n_j - 1)
    def _():
        for t, rb, rows in passes():
            o_ref[t, rb, :] = x_ref[t, rb, :] + gt_ref[rb, :] * _rms(acc_s[rows, :], gpost_ref[...])


def _ffn_weight_specs(l, fc):
    n_j = D_FF // fc
    return [
        pl.BlockSpec((D_MODEL, fc), lambda i, j: (0, j)),
        pl.BlockSpec((D_MODEL, fc), lambda i, j: (0, n_j + j)),
        pl.BlockSpec((None, 3, fc), lambda i, j: (l, 0, j)),
        pl.BlockSpec((None, 3, fc), lambda i, j: (l, 0, n_j + j)),
        pl.BlockSpec((None, 1, fc), lambda i, j: (l, 0, j)),
        pl.BlockSpec((None, 1, fc), lambda i, j: (l, 0, n_j + j)),
    ]


def _next_layer_cast_specs(l, n_tiles, n_j, weights):
    w_up, w_down, w_in, w_out = weights
    rows = D_MODEL // n_tiles
    in_blocks, out_blocks = CAST_IN_BLOCKS, CAST_OUT_BLOCKS
    assert in_blocks <= n_j and out_blocks <= n_j
    shapes = [
        (w_up, (rows, 2 * D_FF // n_j), lambda i, j: (i, j)),
        (w_down, (D_FF // n_j, rows), lambda i, j: (j, i)),
        (w_in, (rows, IN_COLS // in_blocks), lambda i, j: (i, jnp.minimum(j, in_blocks - 1))),
        (w_out, (rows, D_MODEL // out_blocks), lambda i, j: (i, jnp.minimum(j, out_blocks - 1))),
    ]
    in_specs, out_specs, out_shape = [], [], []
    for w, blk, idx in shapes:
        in_specs.append(pl.BlockSpec((None,) + blk, lambda i, j, idx=idx: (l + 1,) + idx(i, j)))
        out_specs.append(pl.BlockSpec(blk, idx))
        out_shape.append(jax.ShapeDtypeStruct(w.shape[1:], BF16))
    return in_specs, out_specs, out_shape


def _ffn_prompt_call(l, x, mod_p, g_pre, w_up_bf, w_conv, b_conv, w_down_bf, g_post, next_weights):
    n_tok = x.shape[0]
    n_seq = mod_p.shape[1]
    seq = n_tok // n_seq
    tm, fc = FFN_ROWS, FFN_COLS
    tps = seq // tm
    n_j = D_FF // fc
    halo = 8
    cast_in, cast_out, cast_shape = ([], [], [])
    if next_weights is not None:
        cast_in, cast_out, cast_shape = _next_layer_cast_specs(l, n_tok // tm, n_j, next_weights)

    def mod_spec(k):
        return pl.BlockSpec((None, None, 1, D_MODEL), lambda i, j: (l, i // tps, 0, k))

    gpre_a, gpre_s = _layer_vec(g_pre, l)
    gpost_a, gpost_s = _layer_vec(g_post, l)
    b3 = b_conv.reshape(DEPTH, 1, 2 * D_FF)
    outs = pl.pallas_call(
        functools.partial(_ffn_prompt_body, tps, len(cast_in)),
        grid=(n_tok // tm, n_j),
        in_specs=[
            pl.BlockSpec((tm, D_MODEL), lambda i, j: (i, 0), pipeline_mode=pl.Buffered(1)),
            mod_spec(3), mod_spec(4), mod_spec(5),
            gpre_s,
            *_ffn_weight_specs(l, fc),
            pl.BlockSpec((fc, D_MODEL), lambda i, j: (j, 0)),
            gpost_s,
            *cast_in,
        ],
        out_specs=[
            pl.BlockSpec((tm, D_MODEL), lambda i, j: (i, 0)),
            pl.BlockSpec((None, 2, fc), lambda i, j: (i, 0, j)),
            pl.BlockSpec((None, 2, fc), lambda i, j: (i, 0, j)),
            *cast_out,
        ],
        out_shape=[
            jax.ShapeDtypeStruct((n_tok, D_MODEL), F32),
            jax.ShapeDtypeStruct((n_tok // tm, 2, D_FF), F32),
            jax.ShapeDtypeStruct((n_tok // tm, 2, D_FF), F32),
            *cast_shape,
        ],
        scratch_shapes=[
            pltpu.VMEM((tm, D_MODEL), BF16),
            pltpu.VMEM((halo + tm, fc), F32),
            pltpu.VMEM((halo + tm, fc), F32),
            pltpu.VMEM((tm, fc), BF16),
            pltpu.VMEM((n_j, halo, fc), F32),
            pltpu.VMEM((n_j, halo, fc), F32),
        ],
        compiler_params=pltpu.CompilerParams(
            dimension_semantics=("arbitrary", "arbitrary"),
            vmem_limit_bytes=FFN_PROMPT_VMEM_LIMIT_BYTES),
        name="ffn_prompt",
    )(x, mod_p, mod_p, mod_p, gpre_a, w_up_bf, w_up_bf, w_conv, w_conv, b3, b3, w_down_bf, gpost_a,
      *(next_weights or ()))
    return outs[0], outs[1], outs[2], tuple(outs[3:])


def _ffn_sample_call(l, x, mod, state_ffn, g_pre, w_up_bf, w_conv, b_conv, w_down_bf, g_post):
    n_t, n_b, _ = x.shape
    fc = FFN_COLS
    bb = FFN_SAMPLE_ROWS // n_t
    tm = n_t * bb
    n_j = D_FF // fc

    def mod_spec(k):
        return pl.BlockSpec((None, bb, D_MODEL), lambda i, j: (l, i, k))

    def past_spec(half):
        return pl.BlockSpec((bb, None, 2, fc), lambda i, j: (i, l, 0, half * n_j + j))

    gpre_a, gpre_s = _layer_vec(g_pre, l)
    gpost_a, gpost_s = _layer_vec(g_post, l)
    b3 = b_conv.reshape(DEPTH, 1, 2 * D_FF)
    return pl.pallas_call(
        _ffn_sample_body,
        grid=(n_b // bb, n_j),
        in_specs=[
            pl.BlockSpec((n_t, bb, D_MODEL), lambda i, j: (0, i, 0)),
            mod_spec(3), mod_spec(4), mod_spec(5),
            gpre_s,
            *_ffn_weight_specs(l, fc),
            past_spec(0), past_spec(1),
            pl.BlockSpec((fc, D_MODEL), lambda i, j: (j, 0)),
            gpost_s,
        ],
        out_specs=[
            pl.BlockSpec((n_t, bb, D_MODEL), lambda i, j: (0, i, 0)),
            pl.BlockSpec((2, bb, fc), lambda i, j: (0, i, j)),
            pl.BlockSpec((2, bb, fc), lambda i, j: (0, i, j)),
        ],
        out_shape=[
            jax.ShapeDtypeStruct((n_t, n_b, D_MODEL), F32),
            jax.ShapeDtypeStruct((2, n_b, D_FF), F32),
            jax.ShapeDtypeStruct((2, n_b, D_FF), F32),
        ],
        scratch_shapes=[
            pltpu.VMEM((tm, D_MODEL), BF16),
            pltpu.VMEM((2 * bb + tm, fc), F32),
            pltpu.VMEM((2 * bb + tm, fc), F32),
            pltpu.VMEM((tm, fc), BF16),
            pltpu.VMEM((tm, D_MODEL), F32),
        ],
        compiler_params=pltpu.CompilerParams(
            dimension_semantics=("arbitrary", "arbitrary"), vmem_limit_bytes=VMEM_LIMIT_BYTES),
        name="ffn_sample",
    )(x, mod, mod, mod, gpre_a, w_up_bf, w_up_bf, w_conv, w_conv, b3, b3,
      state_ffn, state_ffn, w_down_bf, gpost_a)


def kernel(x_prompt, x_sample, state_conv_mix, state_conv_ffn, c_prompt, c_sample, w_ada, b_ada, g_pre_mix, g_post_mix, g_pre_ffn, g_post_ffn, w_in, g_v, w_spatial, b_spatial, w_conv_mix, g_out_a, g_out_b, w_out, w_up, w_conv_ffn, b_conv_ffn, w_down):
    n_seq, seq, _ = x_prompt.shape
    n_b, n_t, _ = x_sample.shape

    pad = (-(n_b + n_seq)) % 8
    c_all = jnp.concatenate([c_sample, c_prompt, jnp.zeros((pad, D_MODEL), F32)], axis=0)
    mod = _ada_call(c_all, w_ada, b_ada)
    mod_p = mod[:, n_b:n_b + n_seq].reshape(DEPTH, n_seq, 1, N_MOD * D_MODEL)

    w_up_bf, w_down_bf, w_in_bf, w_out_bf = [w[0].astype(BF16) for w in (w_up, w_down, w_in, w_out)]
    b_spatial_t = jnp.swapaxes(b_spatial, 1, 2)
    wsp_flat = w_spatial[:, :, :n_t, :n_t].reshape(-1)
    bsp_flat = b_spatial[:, :, :n_t].reshape(-1)
    state_mix = state_conv_mix.reshape(n_b, DEPTH * 2 * W_B)

    xp = x_prompt.reshape(n_seq * seq, D_MODEL)
    xs = jnp.swapaxes(x_sample, 0, 1)
    mix_p, mix_s, ffn_p, ffn_s, vr_p, vr_s = [], [], [], [], [], []
    for l in range(DEPTH):
        xp, nm, vr = _mixer_prompt_call(l, xp, mod_p, g_pre_mix, w_in_bf, g_v, w_spatial,
                                        b_spatial_t, w_conv_mix, g_out_a, g_out_b, w_out_bf,
                                        g_post_mix)
        mix_p.append(nm)
        vr_p.append(vr)
        xs, nm, vr = _mixer_sample_call(l, xs, mod, state_mix, g_pre_mix, w_in_bf, g_v, wsp_flat,
                                        bsp_flat, w_conv_mix, g_out_a, g_out_b, w_out_bf,
                                        g_post_mix)
        mix_s.append(jnp.swapaxes(nm, 0, 1))
        vr_s.append(jnp.swapaxes(vr, 0, 1))
        xs, nfg, nfv = _ffn_sample_call(l, xs, mod, state_conv_ffn, g_pre_ffn, w_up_bf, w_conv_ffn,
                                        b_conv_ffn, w_down_bf, g_post_ffn)
        ffn_s.append(jnp.swapaxes(jnp.concatenate([nfg, nfv], axis=-1), 0, 1))
        next_weights = (w_up, w_down, w_in, w_out) if l + 1 < DEPTH else None
        xp, nfg, nfv, casts = _ffn_prompt_call(l, xp, mod_p, g_pre_ffn, w_up_bf, w_conv_ffn,
                                               b_conv_ffn, w_down_bf, g_post_ffn, next_weights)
        tps = nfg.shape[0] // n_seq
        ffn_p.append(jnp.concatenate([nfg, nfv], axis=-1)[tps - 1::tps])
        if casts:
            w_up_bf, w_down_bf, w_in_bf, w_out_bf = casts

    return (xp.reshape(n_seq, seq, D_MODEL), jnp.swapaxes(xs, 0, 1),
            jnp.stack(mix_p, axis=1), jnp.stack(mix_s, axis=1),
            jnp.stack(ffn_p, axis=1), jnp.stack(ffn_s, axis=1),
            jnp.stack(vr_p, axis=1), jnp.stack(vr_s, axis=1))
```

```python
import functools

import jax
import jax.numpy as jnp
from jax import lax
from jax.experimental import pallas as pl
from jax.experimental.pallas import tpu as pltpu

F32 = jnp.float32
BF16 = jnp.bfloat16

D_MODEL = 2048
DEPTH = 4
CHUNK = 128
N_HEADS_A = 8
HEAD_DIM_A = 128
W_A = N_HEADS_A * HEAD_DIM_A
W_B = D_MODEL - W_A
D_FF = 5632
N_MOD = 6
IN_COLS = 2 * W_A + 3 * W_B
EPS = 1e-6

MIX_ROWS = 256
FFN_ROWS = 512
FFN_COLS = 512
NORM_ROWS = 32
CONV_ROWS = 64
FFN_ROW_BLOCK = 128
FFN_UP_AHEAD = 2
DOWN_COLS = 512
CAST_IN_BLOCKS = 10
CAST_OUT_BLOCKS = 8
ADA_COLS = 2048
VMEM_LIMIT_BYTES = 56 * 1024 * 1024


def _rms(x, g):
    ms = jnp.mean(x * x, axis=-1, keepdims=True)
    return x * lax.rsqrt(ms + EPS) * g


def _gelu(x):
    c = 0.7978845608028654
    return 0.5 * x * (1.0 + jnp.tanh(c * (x + 0.044715 * (x * x * x))))


def _silu(x):
    h = 0.5 * x
    return h + h * jnp.tanh(h)


def _dot(a, b):
    return jnp.dot(a, b, preferred_element_type=F32)


def _conv_rows(u, prev2, prev1, w_ref, b_ref):
    rows = lax.broadcasted_iota(jnp.int32, u.shape, 0)
    p1 = jnp.where(rows == 0, prev1, pltpu.roll(u, 1, 0))
    p2 = jnp.where(rows == 0, prev2, jnp.where(rows == 1, prev1, pltpu.roll(u, 2, 0)))
    y = w_ref[0:1, :] * p2 + w_ref[1:2, :] * p1 + w_ref[2:3, :] * u
    if b_ref is not None:
        y = y + b_ref[...]
    return y


def _conv_blocks(u, past0, past1, w_ref, b_ref, bb):
    t = u.shape[0]
    p1 = jnp.concatenate([past1, u[: t - bb]], axis=0)
    p2 = jnp.concatenate([past0, past1, u[: t - 2 * bb]], axis=0)
    y = w_ref[0:1, :] * p2 + w_ref[1:2, :] * p1 + w_ref[2:3, :] * u
    if b_ref is not None:
        y = y + b_ref[...]
    return y


def _ada_body(c_ref, w_ref, b_ref, o_ref):
    s = _silu(c_ref[...]).astype(BF16)
    o_ref[...] = _dot(s, w_ref[...].astype(BF16)) + b_ref[...]


def _ada_call(c_all, w_ada, b_ada):
    rows = c_all.shape[0]
    n_col = (N_MOD * D_MODEL) // ADA_COLS
    return pl.pallas_call(
        _ada_body,
        grid=(DEPTH, n_col),
        in_specs=[
            pl.BlockSpec((rows, D_MODEL), lambda l, n: (0, 0)),
            pl.BlockSpec((None, D_MODEL, ADA_COLS), lambda l, n: (l, 0, n)),
            pl.BlockSpec((None, 1, ADA_COLS), lambda l, n: (l, 0, n)),
        ],
        out_specs=pl.BlockSpec((None, rows, ADA_COLS), lambda l, n: (l, 0, n)),
        out_shape=jax.ShapeDtypeStruct((DEPTH, rows, N_MOD * D_MODEL), F32),
        compiler_params=pltpu.CompilerParams(
            dimension_semantics=("arbitrary", "arbitrary"),
            vmem_limit_bytes=VMEM_LIMIT_BYTES),
        name="ada_mod",
    )(c_all, w_ada, b_ada.reshape(DEPTH, 1, N_MOD * D_MODEL))


def _mixer_core(x, sh, sc, gt, gpre_ref, w_in_ref, g_v_ref, g_oa_ref, g_ob_ref, w_out_ref,
                gpost_ref, oa_s, spatial_fn, conv_fn):
    h = (_rms(x, gpre_ref[...] * (1.0 + sc)) + sh).astype(BF16)
    u = _gelu(_dot(h, w_in_ref[:, 0:W_A]))
    v = _rms(_gelu(_dot(h, w_in_ref[:, W_A:2 * W_A])), g_v_ref[...])
    spatial_fn(u, v)
    ra = _rms(oa_s[...], g_oa_ref[...]).astype(BF16)
    gate_b = _dot(h, w_in_ref[:, 2 * W_A:2 * W_A + W_B])
    gate_c = _dot(h, w_in_ref[:, 2 * W_A + W_B:2 * W_A + 2 * W_B])
    h_b = _dot(h, w_in_ref[:, 2 * W_A + 2 * W_B:])
    ci = gate_c * h_b
    rb = _rms(gate_b * conv_fn(ci), g_ob_ref[...]).astype(BF16)
    merged = _dot(ra, w_out_ref[0:W_A, :]) + _dot(rb, w_out_ref[W_A:, :])
    y = x + _rms(merged, gt * gpost_ref[...])
    return y, v, ci


def _mixer_prompt_body(tiles_per_seq, x_ref, sh_ref, sc_ref, gt_ref, gpre_ref, w_in_ref, g_v_ref,
                       wsp_ref, bsp_ref, wconv_ref, g_oa_ref, g_ob_ref, w_out_ref, gpost_ref,
                       y_ref, nmix_ref, vrows_ref, oa_s, carry_s):
    i = pl.program_id(0)
    tm = x_ref.shape[0]
    first = (i % tiles_per_seq) == 0

    def spatial_fn(u, v):
        vb = v.astype(BF16)
        tril = (lax.broadcasted_iota(jnp.int32, (CHUNK, CHUNK), 0)
                >= lax.broadcasted_iota(jnp.int32, (CHUNK, CHUNK), 1))
        for hd in range(N_HEADS_A):
            cols = slice(hd * HEAD_DIM_A, (hd + 1) * HEAD_DIM_A)
            wm = jnp.where(tril, wsp_ref[hd], 0.0).astype(BF16)
            bias = bsp_ref[:, hd:hd + 1]
            for c in range(tm // CHUNK):
                rows = slice(c * CHUNK, (c + 1) * CHUNK)
                mixed = _dot(wm, vb[rows, cols]) + bias
                oa_s[rows, cols] = u[rows, cols] * mixed

    def conv_fn(ci):
        prev2 = jnp.where(first, 0.0, carry_s[6:7, :])
        prev1 = jnp.where(first, 0.0, carry_s[7:8, :])
        return _conv_rows(ci, prev2, prev1, wconv_ref, None)

    y, v, ci = _mixer_core(x_ref[...], sh_ref[...], sc_ref[...], gt_ref[...], gpre_ref, w_in_ref,
                           g_v_ref, g_oa_ref, g_ob_ref, w_out_ref, gpost_ref, oa_s,
                           spatial_fn, conv_fn)
    y_ref[...] = y
    carry_s[...] = ci[tm - 8:, :]
    nmix_ref[...] = ci[tm - 2:, :]
    vrows_ref[...] = v[tm - CHUNK:, :]


def _mixer_sample_body(layer, x_ref, sh_ref, sc_ref, gt_ref, gpre_ref, w_in_ref, g_v_ref,
                       wsp_ref, bsp_ref, wconv_ref, past0_ref, past1_ref, g_oa_ref, g_ob_ref,
                       w_out_ref, gpost_ref, y_ref, nmix_ref, vrows_ref, oa_s):
    n_t, bb, _ = x_ref.shape
    tm = n_t * bb

    def rep(m_ref):
        return jnp.concatenate([m_ref[...]] * n_t, axis=0)

    def spatial_fn(u, v):
        for hd in range(N_HEADS_A):
            cols = slice(hd * HEAD_DIM_A, (hd + 1) * HEAD_DIM_A)
            base = (layer * N_HEADS_A + hd) * n_t
            for t in range(n_t):
                mixed = None
                for s in range(t + 1):
                    term = wsp_ref[(base + t) * n_t + s] * v[s * bb:(s + 1) * bb, cols]
                    mixed = term if mixed is None else mixed + term
                mixed = mixed + bsp_ref[base + t]
                oa_s[t * bb:(t + 1) * bb, cols] = u[t * bb:(t + 1) * bb, cols] * mixed

    def conv_fn(ci):
        return _conv_blocks(ci, past0_ref[...], past1_ref[...], wconv_ref, None, bb)

    x = x_ref[...].reshape(tm, D_MODEL)
    y, v, ci = _mixer_core(x, rep(sh_ref), rep(sc_ref), rep(gt_ref), gpre_ref, w_in_ref,
                           g_v_ref, g_oa_ref, g_ob_ref, w_out_ref, gpost_ref, oa_s,
                           spatial_fn, conv_fn)
    y_ref[...] = y.reshape(n_t, bb, D_MODEL)
    nmix_ref[...] = ci[tm - 2 * bb:, :].reshape(2, bb, W_B)
    vrows_ref[...] = v.reshape(n_t, bb, W_A)


def _layer_vec(arr, l):
    c = arr.shape[-1]
    return arr.reshape(DEPTH, 1, c), pl.BlockSpec((None, 1, c), lambda *_: (l, 0, 0))


def _resident(shape, index):
    return pl.BlockSpec(shape, lambda *_: index, pipeline_mode=pl.Buffered(1))


def _mixer_prompt_call(l, x, mod_p, g_pre, w_in_bf, g_v, w_spatial, b_spatial_t, w_conv, g_oa, g_ob,
                       w_out_bf, g_post):
    n_tok = x.shape[0]
    n_seq = mod_p.shape[1]
    seq = n_tok // n_seq
    tm = MIX_ROWS
    tps = seq // tm

    def mod_spec(k):
        return pl.BlockSpec((None, None, 1, D_MODEL), lambda i: (l, i // tps, 0, k))

    gpre_a, gpre_s = _layer_vec(g_pre, l)
    gv_a, gv_s = _layer_vec(g_v, l)
    goa_a, goa_s = _layer_vec(g_oa, l)
    gob_a, gob_s = _layer_vec(g_ob, l)
    gpost_a, gpost_s = _layer_vec(g_post, l)
    return pl.pallas_call(
        functools.partial(_mixer_prompt_body, tps),
        grid=(n_tok // tm,),
        in_specs=[
            pl.BlockSpec((tm, D_MODEL), lambda i: (i, 0)),
            mod_spec(0), mod_spec(1), mod_spec(2),
            gpre_s,
            _resident((D_MODEL, IN_COLS), (0, 0)),
            gv_s,
            pl.BlockSpec((None, N_HEADS_A, CHUNK, CHUNK), lambda i: (l, 0, 0, 0)),
            pl.BlockSpec((None, CHUNK, N_HEADS_A), lambda i: (l, 0, 0)),
            pl.BlockSpec((None, 3, W_B), lambda i: (l, 0, 0)),
            goa_s, gob_s,
            _resident((D_MODEL, D_MODEL), (0, 0)),
            gpost_s,
        ],
        out_specs=[
            pl.BlockSpec((tm, D_MODEL), lambda i: (i, 0)),
            pl.BlockSpec((None, 2, W_B), lambda i: (i // tps, 0, 0)),
            pl.BlockSpec((None, CHUNK, W_A), lambda i: (i // tps, 0, 0)),
        ],
        out_shape=[
            jax.ShapeDtypeStruct((n_tok, D_MODEL), F32),
            jax.ShapeDtypeStruct((n_seq, 2, W_B), F32),
            jax.ShapeDtypeStruct((n_seq, CHUNK, W_A), F32),
        ],
        scratch_shapes=[pltpu.VMEM((tm, W_A), F32), pltpu.VMEM((8, W_B), F32)],
        compiler_params=pltpu.CompilerParams(
            dimension_semantics=("arbitrary",), vmem_limit_bytes=VMEM_LIMIT_BYTES),
        name="mixer_prompt",
    )(x, mod_p, mod_p, mod_p, gpre_a, w_in_bf, gv_a, w_spatial, b_spatial_t, w_conv, goa_a, gob_a,
      w_out_bf, gpost_a)


def _mixer_sample_call(l, x, mod, state_mix, g_pre, w_in_bf, g_v, wsp_flat, bsp_flat, w_conv, g_oa,
                       g_ob, w_out_bf, g_post):
    n_t, n_b, _ = x.shape
    bb = MIX_ROWS // n_t
    tm = n_t * bb

    def mod_spec(k):
        return pl.BlockSpec((None, bb, D_MODEL), lambda i: (l, i, k))

    def past_spec(r):
        return pl.BlockSpec((bb, W_B), lambda i: (i, l * 2 + r))

    gpre_a, gpre_s = _layer_vec(g_pre, l)
    gv_a, gv_s = _layer_vec(g_v, l)
    goa_a, goa_s = _layer_vec(g_oa, l)
    gob_a, gob_s = _layer_vec(g_ob, l)
    gpost_a, gpost_s = _layer_vec(g_post, l)
    smem = pl.BlockSpec(memory_space=pltpu.SMEM)
    return pl.pallas_call(
        functools.partial(_mixer_sample_body, l),
        grid=(n_b // bb,),
        in_specs=[
            pl.BlockSpec((n_t, bb, D_MODEL), lambda i: (0, i, 0)),
            mod_spec(0), mod_spec(1), mod_spec(2),
            gpre_s,
            _resident((D_MODEL, IN_COLS), (0, 0)),
            gv_s,
            smem, smem,
            pl.BlockSpec((None, 3, W_B), lambda i: (l, 0, 0)),
            past_spec(0), past_spec(1),
            goa_s, gob_s,
            _resident((D_MODEL, D_MODEL), (0, 0)),
            gpost_s,
        ],
        out_specs=[
            pl.BlockSpec((n_t, bb, D_MODEL), lambda i: (0, i, 0)),
            pl.BlockSpec((2, bb, W_B), lambda i: (0, i, 0)),
            pl.BlockSpec((n_t, bb, W_A), lambda i: (0, i, 0)),
        ],
        out_shape=[
            jax.ShapeDtypeStruct((n_t, n_b, D_MODEL), F32),
            jax.ShapeDtypeStruct((2, n_b, W_B), F32),
            jax.ShapeDtypeStruct((n_t, n_b, W_A), F32),
        ],
        scratch_shapes=[pltpu.VMEM((tm, W_A), F32)],
        compiler_params=pltpu.CompilerParams(
            dimension_semantics=("arbitrary",), vmem_limit_bytes=VMEM_LIMIT_BYTES),
        name="mixer_sample",
    )(x, mod, mod, mod, gpre_a, w_in_bf, gv_a, wsp_flat, bsp_flat, w_conv, state_mix, state_mix,
      goa_a, gob_a, w_out_bf, gpost_a)


def _ffn_chunk_step(tm, halo, step, wg_ref, wv_ref, cwg_ref, cwv_ref, cbg_ref, cbv_ref, wd_ref,
                    h_s, upg_s, upv_s, act_s, acc_add):
    n_k = tm // FFN_ROW_BLOCK

    def up(k):
        rows = slice(k * FFN_ROW_BLOCK, (k + 1) * FFN_ROW_BLOCK)
        dst = slice(halo + k * FFN_ROW_BLOCK, halo + (k + 1) * FFN_ROW_BLOCK)
        h = h_s[rows, :]
        upg_s[dst, :] = _dot(h, wg_ref[...])
        upv_s[dst, :] = _dot(h, wv_ref[...])

    def conv(up_s, w_ref, b_ref, r, n):
        t2, t1, t0 = [up_s[pl.ds(halo - k * step + r, n), :] for k in (2, 1, 0)]
        return (w_ref[0:1, :] * t2 + w_ref[1:2, :] * t1 + w_ref[2:3, :] * t0) + b_ref[...]

    def gate_down(k):
        for r in range(k * FFN_ROW_BLOCK, (k + 1) * FFN_ROW_BLOCK, CONV_ROWS):
            cg = conv(upg_s, cwg_ref, cbg_ref, r, CONV_ROWS)
            cv = conv(upv_s, cwv_ref, cbv_ref, r, CONV_ROWS)
            act_s[r:r + CONV_ROWS, :] = (_silu(cg) * cv).astype(BF16)
        rows = slice(k * FFN_ROW_BLOCK, (k + 1) * FFN_ROW_BLOCK)
        for c in range(0, D_MODEL, DOWN_COLS):
            acc_add(k, c, _dot(act_s[rows, :], wd_ref[:, c:c + DOWN_COLS]))

    for k in range(min(FFN_UP_AHEAD, n_k)):
        up(k)
    for k in range(n_k):
        if k + FFN_UP_AHEAD < n_k:
            up(k + FFN_UP_AHEAD)
        gate_down(k)


def _ffn_prompt_body(tiles_per_seq, n_cast, x_ref, sh_ref, sc_ref, gt_ref, gpre_ref, wg_ref, wv_ref,
                     cwg_ref, cwv_ref, cbg_ref, cbv_ref, wd_ref, gpost_ref, *rest):
    cast_in, rest = rest[:n_cast], rest[n_cast:]
    o_ref, nfg_ref, nfv_ref = rest[:3]
    cast_out, rest = rest[3:3 + n_cast], rest[3 + n_cast:]
    h_s, upg_s, upv_s, act_s, carry_g, carry_v = rest
    i = pl.program_id(0)
    j = pl.program_id(1)
    n_j = pl.num_programs(1)
    tm = x_ref.shape[0]
    halo = 8
    first_tile = (i % tiles_per_seq) == 0

    for src, dst in zip(cast_in, cast_out):
        dst[...] = src[...].astype(BF16)

    @pl.when(j == 0)
    def _():
        for r in range(0, tm, NORM_ROWS):
            rows = slice(r, r + NORM_ROWS)
            h_s[rows, :] = (_rms(x_ref[rows, :], gpre_ref[...]) * (1.0 + sc_ref[...])
                            + sh_ref[...]).astype(BF16)
        o_ref[...] = jnp.zeros_like(o_ref)

    upg_s[0:halo, :] = jnp.where(first_tile, 0.0, carry_g[j])
    upv_s[0:halo, :] = jnp.where(first_tile, 0.0, carry_v[j])

    def acc_add(k, c, part):
        rows = slice(k * FFN_ROW_BLOCK, (k + 1) * FFN_ROW_BLOCK)
        o_ref[rows, c:c + part.shape[1]] += part

    _ffn_chunk_step(tm, halo, 1, wg_ref, wv_ref, cwg_ref, cwv_ref, cbg_ref, cbv_ref, wd_ref,
                    h_s, upg_s, upv_s, act_s, acc_add)
    carry_g[j] = upg_s[tm:tm + halo, :]
    carry_v[j] = upv_s[tm:tm + halo, :]
    nfg_ref[...] = upg_s[halo + tm - 2:halo + tm, :]
    nfv_ref[...] = upv_s[halo + tm - 2:halo + tm, :]

    @pl.when(j == n_j - 1)
    def _():
        for r in range(0, tm, NORM_ROWS):
            rows = slice(r, r + NORM_ROWS)
            o_ref[rows, :] = x_ref[rows, :] + gt_ref[...] * _rms(o_ref[rows, :], gpost_ref[...])


def _ffn_sample_body(x_ref, sh_ref, sc_ref, gt_ref, gpre_ref, wg_ref, wv_ref,
                     cwg_ref, cwv_ref, cbg_ref, cbv_ref, pg_ref, pv_ref,
                     wd_ref, gpost_ref, o_ref, nfg_ref, nfv_ref, h_s, upg_s, upv_s, act_s, acc_s):
    j = pl.program_id(1)
    n_j = pl.num_programs(1)
    n_t, bb, _ = x_ref.shape
    tm = n_t * bb
    halo = 2 * bb

    def passes():
        for t in range(n_t):
            for r in range(0, bb, NORM_ROWS):
                yield t, slice(r, r + NORM_ROWS), slice(t * bb + r, t * bb + r + NORM_ROWS)

    @pl.when(j == 0)
    def _():
        for t, rb, rows in passes():
            h_s[rows, :] = (_rms(x_ref[t, rb, :], gpre_ref[...]) * (1.0 + sc_ref[rb, :])
                            + sh_ref[rb, :]).astype(BF16)
        acc_s[...] = jnp.zeros_like(acc_s)

    for r in range(2):
        upg_s[r * bb:(r + 1) * bb, :] = pg_ref[:, r, :]
        upv_s[r * bb:(r + 1) * bb, :] = pv_ref[:, r, :]

    def acc_add(k, c, part):
        rows = slice(k * FFN_ROW_BLOCK, (k + 1) * FFN_ROW_BLOCK)
        acc_s[rows, c:c + part.shape[1]] += part

    _ffn_chunk_step(tm, halo, bb, wg_ref, wv_ref, cwg_ref, cwv_ref, cbg_ref, cbv_ref, wd_ref,
                    h_s, upg_s, upv_s, act_s, acc_add)
    for k in range(2):
        rows = slice(tm + k * bb, tm + (k + 1) * bb)
        nfg_ref[k] = upg_s[rows, :]
        nfv_ref[k] = upv_s[rows, :]

    @pl.when(j == n_j - 1)
    def _():
        for t, rb, rows in passes():
            o_ref[t, rb, :] = x_ref[t, rb, :] + gt_ref[rb, :] * _rms(acc_s[rows, :], gpost_ref[...])


def _ffn_weight_specs(l, fc):
    n_j = D_FF // fc
    return [
        pl.BlockSpec((D_MODEL, fc), lambda i, j: (0, j)),
        pl.BlockSpec((D_MODEL, fc), lambda i, j: (0, n_j + j)),
        pl.BlockSpec((None, 3, fc), lambda i, j: (l, 0, j)),
        pl.BlockSpec((None, 3, fc), lambda i, j: (l, 0, n_j + j)),
        pl.BlockSpec((None, 1, fc), lambda i, j: (l, 0, j)),
        pl.BlockSpec((None, 1, fc), lambda i, j: (l, 0, n_j + j)),
    ]


def _next_layer_cast_specs(l, n_tiles, n_j, weights):
    w_up, w_down, w_in, w_out = weights
    rows = D_MODEL // n_tiles
    in_blocks, out_blocks = CAST_IN_BLOCKS, CAST_OUT_BLOCKS
    assert in_blocks <= n_j and out_blocks <= n_j
    shapes = [
        (w_up, (rows, 2 * D_FF // n_j), lambda i, j: (i, j)),
        (w_down, (D_FF // n_j, rows), lambda i, j: (j, i)),
        (w_in, (rows, IN_COLS // in_blocks), lambda i, j: (i, jnp.minimum(j, in_blocks - 1))),
        (w_out, (rows, D_MODEL // out_blocks), lambda i, j: (i, jnp.minimum(j, out_blocks - 1))),
    ]
    in_specs, out_specs, out_shape = [], [], []
    for w, blk, idx in shapes:
        in_specs.append(pl.BlockSpec((None,) + blk, lambda i, j, idx=idx: (l + 1,) + idx(i, j)))
        out_specs.append(pl.BlockSpec(blk, idx))
        out_shape.append(jax.ShapeDtypeStruct(w.shape[1:], BF16))
    return in_specs, out_specs, out_shape


def _ffn_prompt_call(l, x, mod_p, g_pre, w_up_bf, w_conv, b_conv, w_down_bf, g_post, next_weights):
    n_tok = x.shape[0]
    n_seq = mod_p.shape[1]
    seq = n_tok // n_seq
    tm, fc = FFN_ROWS, FFN_COLS
    tps = seq // tm
    n_j = D_FF // fc
    halo = 8
    cast_in, cast_out, cast_shape = ([], [], [])
    if next_weights is not None:
        cast_in, cast_out, cast_shape = _next_layer_cast_specs(l, n_tok // tm, n_j, next_weights)

    def mod_spec(k):
        return pl.BlockSpec((None, None, 1, D_MODEL), lambda i, j: (l, i // tps, 0, k))

    gpre_a, gpre_s = _layer_vec(g_pre, l)
    gpost_a, gpost_s = _layer_vec(g_post, l)
    b3 = b_conv.reshape(DEPTH, 1, 2 * D_FF)
    outs = pl.pallas_call(
        functools.partial(_ffn_prompt_body, tps, len(cast_in)),
        grid=(n_tok // tm, n_j),
        in_specs=[
            pl.BlockSpec((tm, D_MODEL), lambda i, j: (i, 0)),
            mod_spec(3), mod_spec(4), mod_spec(5),
            gpre_s,
            *_ffn_weight_specs(l, fc),
            pl.BlockSpec((fc, D_MODEL), lambda i, j: (j, 0)),
            gpost_s,
            *cast_in,
        ],
        out_specs=[
            pl.BlockSpec((tm, D_MODEL), lambda i, j: (i, 0)),
            pl.BlockSpec((None, 2, fc), lambda i, j: (i, 0, j)),
            pl.BlockSpec((None, 2, fc), lambda i, j: (i, 0, j)),
            *cast_out,
        ],
        out_shape=[
            jax.ShapeDtypeStruct((n_tok, D_MODEL), F32),
            jax.ShapeDtypeStruct((n_tok // tm, 2, D_FF), F32),
            jax.ShapeDtypeStruct((n_tok // tm, 2, D_FF), F32),
            *cast_shape,
        ],
        scratch_shapes=[
            pltpu.VMEM((tm, D_MODEL), BF16),
            pltpu.VMEM((halo + tm, fc), F32),
            pltpu.VMEM((halo + tm, fc), F32),
            pltpu.VMEM((tm, fc), BF16),
            pltpu.VMEM((n_j, halo, fc), F32),
            pltpu.VMEM((n_j, halo, fc), F32),
        ],
        compiler_params=pltpu.CompilerParams(
            dimension_semantics=("arbitrary", "arbitrary"), vmem_limit_bytes=VMEM_LIMIT_BYTES),
        name="ffn_prompt",
    )(x, mod_p, mod_p, mod_p, gpre_a, w_up_bf, w_up_bf, w_conv, w_conv, b3, b3, w_down_bf, gpost_a,
      *(next_weights or ()))
    return outs[0], outs[1], outs[2], tuple(outs[3:])


def _ffn_sample_call(l, x, mod, state_ffn, g_pre, w_up_bf, w_conv, b_conv, w_down_bf, g_post):
    n_t, n_b, _ = x.shape
    fc = FFN_COLS
    bb = FFN_ROWS // n_t
    tm = n_t * bb
    n_j = D_FF // fc

    def mod_spec(k):
        return pl.BlockSpec((None, bb, D_MODEL), lambda i, j: (l, i, k))

    def past_spec(half):
        return pl.BlockSpec((bb, None, 2, fc), lambda i, j: (i, l, 0, half * n_j + j))

    gpre_a, gpre_s = _layer_vec(g_pre, l)
    gpost_a, gpost_s = _layer_vec(g_post, l)
    b3 = b_conv.reshape(DEPTH, 1, 2 * D_FF)
    return pl.pallas_call(
        _ffn_sample_body,
        grid=(n_b // bb, n_j),
        in_specs=[
            pl.BlockSpec((n_t, bb, D_MODEL), lambda i, j: (0, i, 0)),
            mod_spec(3), mod_spec(4), mod_spec(5),
            gpre_s,
            *_ffn_weight_specs(l, fc),
            past_spec(0), past_spec(1),
            pl.BlockSpec((fc, D_MODEL), lambda i, j: (j, 0)),
            gpost_s,
        ],
        out_specs=[
            pl.BlockSpec((n_t, bb, D_MODEL), lambda i, j: (0, i, 0)),
            pl.BlockSpec((2, bb, fc), lambda i, j: (0, i, j)),
            pl.BlockSpec((2, bb, fc), lambda i, j: (0, i, j)),
        ],
        out_shape=[
            jax.ShapeDtypeStruct((n_t, n_b, D_MODEL), F32),
            jax.ShapeDtypeStruct((2, n_b, D_FF), F32),
            jax.ShapeDtypeStruct((2, n_b, D_FF), F32),
        ],
        scratch_shapes=[
            pltpu.VMEM((tm, D_MODEL), BF16),
            pltpu.VMEM((2 * bb + tm, fc), F32),
            pltpu.VMEM((2 * bb + tm, fc), F32),
            pltpu.VMEM((tm, fc), BF16),
            pltpu.VMEM((tm, D_MODEL), F32),
        ],
        compiler_params=pltpu.CompilerParams(
            dimension_semantics=("arbitrary", "arbitrary"), vmem_limit_bytes=VMEM_LIMIT_BYTES),
        name="ffn_sample",
    )(x, mod, mod, mod, gpre_a, w_up_bf, w_up_bf, w_conv, w_conv, b3, b3,
      state_ffn, state_ffn, w_down_bf, gpost_a)


def kernel(x_prompt, x_sample, state_conv_mix, state_conv_ffn, c_prompt, c_sample, w_ada, b_ada, g_pre_mix, g_post_mix, g_pre_ffn, g_post_ffn, w_in, g_v, w_spatial, b_spatial, w_conv_mix, g_out_a, g_out_b, w_out, w_up, w_conv_ffn, b_conv_ffn, w_down):
    n_seq, seq, _ = x_prompt.shape
    n_b, n_t, _ = x_sample.shape

    pad = (-(n_b + n_seq)) % 8
    c_all = jnp.concatenate([c_sample, c_prompt, jnp.zeros((pad, D_MODEL), F32)], axis=0)
    mod = _ada_call(c_all, w_ada, b_ada)
    mod_p = mod[:, n_b:n_b + n_seq].reshape(DEPTH, n_seq, 1, N_MOD * D_MODEL)

    w_up_bf, w_down_bf, w_in_bf, w_out_bf = [w[0].astype(BF16) for w in (w_up, w_down, w_in, w_out)]
    b_spatial_t = jnp.swapaxes(b_spatial, 1, 2)
    wsp_flat = w_spatial[:, :, :n_t, :n_t].reshape(-1)
    bsp_flat = b_spatial[:, :, :n_t].reshape(-1)
    state_mix = state_conv_mix.reshape(n_b, DEPTH * 2 * W_B)

    xp = x_prompt.reshape(n_seq * seq, D_MODEL)
    xs = jnp.swapaxes(x_sample, 0, 1)
    mix_p, mix_s, ffn_p, ffn_s, vr_p, vr_s = [], [], [], [], [], []
    for l in range(DEPTH):
        xp, nm, vr = _mixer_prompt_call(l, xp, mod_p, g_pre_mix, w_in_bf, g_v, w_spatial,
                                        b_spatial_t, w_conv_mix, g_out_a, g_out_b, w_out_bf,
                                        g_post_mix)
        mix_p.append(nm)
        vr_p.append(vr)
        xs, nm, vr = _mixer_sample_call(l, xs, mod, state_mix, g_pre_mix, w_in_bf, g_v, wsp_flat,
                                        bsp_flat, w_conv_mix, g_out_a, g_out_b, w_out_bf,
                                        g_post_mix)
        mix_s.append(jnp.swapaxes(nm, 0, 1))
        vr_s.append(jnp.swapaxes(vr, 0, 1))
        xs, nfg, nfv = _ffn_sample_call(l, xs, mod, state_conv_ffn, g_pre_ffn, w_up_bf, w_conv_ffn,
                                        b_conv_ffn, w_down_bf, g_post_ffn)
        ffn_s.append(jnp.swapaxes(jnp.concatenate([nfg, nfv], axis=-1), 0, 1))
        next_weights = (w_up, w_down, w_in, w_out) if l + 1 < DEPTH else None
        xp, nfg, nfv, casts = _ffn_prompt_call(l, xp, mod_p, g_pre_ffn, w_up_bf, w_conv_ffn,
                                               b_conv_ffn, w_down_bf, g_post_ffn, next_weights)
        tps = nfg.shape[0] // n_seq
        ffn_p.append(jnp.concatenate([nfg, nfv], axis=-1)[tps - 1::tps])
        if casts:
            w_up_bf, w_down_bf, w_in_bf, w_out_bf = casts

    return (xp.reshape(n_seq, seq, D_MODEL), jnp.swapaxes(xs, 0, 1),
            jnp.stack(mix_p, axis=1), jnp.stack(mix_s, axis=1),
            jnp.stack(ffn_p, axis=1), jnp.stack(ffn_s, axis=1),
            jnp.stack(vr_p, axis=1), jnp.stack(vr_s, axis=1))
```
